```python
import math
import jax, jax.numpy as jnp
from jax import lax
import numpy as np

D_MODEL = 2048
BATCH = 4
SEQ = 4096
DEPTH = 4
DEC_BATCH = 2
DEC_SEQ = 4096
PAST_LEN = 128

N_HEADS = 16
N_KV_HEADS = 4
HEAD_DIM = 128
ATTN_W = N_HEADS * HEAD_DIM
KV_W = N_KV_HEADS * HEAD_DIM
WINDOW = 128
BLOCK = 128
ROPE_THETA = 10000.0
SSD_EXPAND = 2
D_INNER = SSD_EXPAND * D_MODEL
SSD_HEAD_DIM = 64
SSD_HEADS = D_INNER // SSD_HEAD_DIM
SSD_GROUPS = 8
D_STATE = 128
D_CONV = 5
CHUNK = 128
CONV_CH = D_INNER + 2 * SSD_GROUPS * D_STATE
D_FF = ((8 * D_MODEL // 3 + 255) // 256) * 256
EPS = 1e-6

IN_SPLITS = [ATTN_W, KV_W, KV_W, D_MODEL, D_MODEL, D_INNER, CONV_CH, 2 * SSD_HEADS]
N_IN = sum(IN_SPLITS)
IN_OFFSETS = list(np.cumsum(IN_SPLITS)[:-1].tolist())

kernel_name = "hybrid_bidir_swa_ssd_encoder"


def rms_norm(x, g):
    xf = x.astype(jnp.float32)
    y = xf * lax.rsqrt(jnp.mean(xf * xf, axis=-1, keepdims=True) + EPS)
    return (y * g.astype(jnp.float32)).astype(x.dtype)


def rope(x):
    l, d = x.shape[1], x.shape[-1]
    half = d // 2
    inv_freq = ROPE_THETA ** (-jnp.arange(half, dtype=jnp.float32) * 2.0 / d)
    ang = jnp.arange(l, dtype=jnp.float32)[:, None] * inv_freq[None, :]
    cos = jnp.cos(ang)[None, :, None, :]
    sin = jnp.sin(ang)[None, :, None, :]
    xf = x.astype(jnp.float32)
    x1, x2 = xf[..., :half], xf[..., half:]
    return jnp.concatenate([x1 * cos - x2 * sin, x2 * cos + x1 * sin], axis=-1).astype(x.dtype)


def windowed_attention(q, k, v, sink):
    b, l = q.shape[0], q.shape[1]
    nb = l // BLOCK
    g = N_HEADS // N_KV_HEADS
    qb = q.reshape(b, nb, BLOCK, N_KV_HEADS, g, HEAD_DIM)

    def windows(t):
        tp = jnp.pad(t, ((0, 0), (BLOCK, BLOCK), (0, 0), (0, 0)))
        tp = tp.reshape(b, nb + 2, BLOCK, N_KV_HEADS, HEAD_DIM)
        return jnp.concatenate([tp[:, :-2], tp[:, 1:-1], tp[:, 2:]], axis=2)

    kw, vw = windows(k), windows(v)
    s = jnp.einsum('bnqhgd,bnkhd->bnhgqk', qb, kw).astype(jnp.float32) * (HEAD_DIM ** -0.5)
    blk = jnp.arange(nb)[:, None, None]
    t = jnp.arange(BLOCK)[None, :, None]
    sk = jnp.arange(3 * BLOCK)[None, None, :]
    rel = sk - BLOCK - t
    kpos = (blk - 1) * BLOCK + sk
    valid = (jnp.abs(rel) <= WINDOW) & (kpos >= 0) & (kpos < l)
    s = jnp.where(valid[None, :, None, None], s, -jnp.inf)
    sink_l = sink.astype(jnp.float32).reshape(N_KV_HEADS, g)[None, None, :, :, None, None]
    m = jnp.maximum(jnp.max(s, axis=-1, keepdims=True), sink_l)
    p = jnp.exp(s - m)
    denom = jnp.sum(p, axis=-1, keepdims=True) + jnp.exp(sink_l - m)
    p = (p / denom).astype(v.dtype)
    o = jnp.einsum('bnhgqk,bnkhd->bnqhgd', p, vw)
    return o.reshape(b, l, ATTN_W)


def ssd_scan(x, dt, A, B, C):
    b, l, h, p = x.shape
    g, n = B.shape[2], B.shape[3]
    hg = h // g
    c = l // CHUNK
    xd = (x.astype(jnp.float32) * dt[..., None]).reshape(b, c, CHUNK, g, hg, p)
    a = (dt * A).reshape(b, c, CHUNK, g, hg).transpose(0, 1, 3, 4, 2)
    a_cum = jnp.cumsum(a, axis=-1)
    Bc = B.astype(jnp.float32).reshape(b, c, CHUNK, g, n)
    Cc = C.astype(jnp.float32).reshape(b, c, CHUNK, g, n)
    causal = jnp.tril(jnp.ones((CHUNK, CHUNK), dtype=bool))
    seg = a_cum[..., :, None] - a_cum[..., None, :]
    decay = jnp.exp(jnp.where(causal, seg, -jnp.inf))
    CB = jnp.einsum('bclgn,bcsgn->bcgls', Cc, Bc)
    M = CB[:, :, :, None] * decay
    y_diag = jnp.einsum('bcghls,bcsghp->bclghp', M, xd)
    ds = jnp.moveaxis(jnp.exp(a_cum[..., -1:] - a_cum), -1, 2)
    states = jnp.einsum('bcsgn,bcsghp->bcghpn', Bc, xd * ds[..., None])
    chunk_decay = jnp.exp(a_cum[..., -1])

    def step(hstate, inp):
        st, dec = inp
        return hstate * dec[..., None, None] + st, hstate

    init = jnp.zeros((b, g, hg, p, n), jnp.float32)
    _, prev = lax.scan(step, init, (jnp.moveaxis(states, 1, 0), jnp.moveaxis(chunk_decay, 1, 0)))
    prev = jnp.moveaxis(prev, 0, 1)
    out_decay = jnp.moveaxis(jnp.exp(a_cum), -1, 2)
    y_off = jnp.einsum('bclgn,bcghpn->bclghp', Cc, prev) * out_decay[..., None]
    return (y_diag + y_off).reshape(b, l, h, p)


def ssd_mixer(z, xbc, dt_raw, conv_w, conv_b, a_log, dt_bias, d_skip, norm_g):
    b, l = z.shape[0], z.shape[1]
    xbc = lax.conv_general_dilated(xbc, conv_w[:, None, :].astype(xbc.dtype), window_strides=(1,),
                                   padding=[(D_CONV // 2, D_CONV // 2)],
                                   dimension_numbers=('NWC', 'WIO', 'NWC'),
                                   feature_group_count=CONV_CH) + conv_b
    xbc = jax.nn.silu(xbc)
    gn = SSD_GROUPS * D_STATE
    xs = xbc[..., :D_INNER].reshape(b, l, SSD_HEADS, SSD_HEAD_DIM)
    Bm = xbc[..., D_INNER:D_INNER + gn].reshape(b, l, SSD_GROUPS, D_STATE)
    Cm = xbc[..., D_INNER + gn:].reshape(b, l, SSD_GROUPS, D_STATE)
    dt = jax.nn.softplus(dt_raw.reshape(b, l, 2, SSD_HEADS).astype(jnp.float32) + dt_bias.astype(jnp.float32))
    A = -jnp.exp(a_log.astype(jnp.float32))
    y_f = ssd_scan(xs, dt[:, :, 0], A[0], Bm, Cm)
    fl = lambda t: jnp.flip(t, axis=1)
    y_b = fl(ssd_scan(fl(xs), fl(dt[:, :, 1]), A[1], fl(Bm), fl(Cm)))
    y = y_f + y_b + xs.astype(jnp.float32) * d_skip.astype(jnp.float32)[:, None]
    y = y.reshape(b, l, D_INNER) * jax.nn.silu(z.astype(jnp.float32))
    yg = y.reshape(b, l, SSD_GROUPS, D_INNER // SSD_GROUPS)
    yg = yg * lax.rsqrt(jnp.mean(yg * yg, axis=-1, keepdims=True) + EPS)
    return (yg.reshape(b, l, D_INNER) * norm_g.astype(jnp.float32)).astype(z.dtype)


def trunk(x, w_in, conv_w, conv_b, a_log, dt_bias, d_skip, ssd_norm, attn_sink,
          w_out_attn, w_out_ssd, w_out, norm_mix, norm_ffn, w_gate_up, w_down, final_norm):
    b, l = x.shape[0], x.shape[1]
    for i in range(DEPTH):
        h = rms_norm(x, norm_mix[i])
        proj = h @ w_in[i]
        q, k, v, g_att, g_ssd, z, xbc, dt_raw = jnp.split(proj, IN_OFFSETS, axis=-1)
        q = rope(q.reshape(b, l, N_HEADS, HEAD_DIM))
        k = rope(k.reshape(b, l, N_KV_HEADS, HEAD_DIM))
        v = v.reshape(b, l, N_KV_HEADS, HEAD_DIM)
        attn = windowed_attention(q, k, v, attn_sink[i])
        ssd = ssd_mixer(z, xbc, dt_raw, conv_w[i], conv_b[i], a_log[i], dt_bias[i], d_skip[i], ssd_norm[i])
        mix = jax.nn.sigmoid(g_att) * (attn @ w_out_attn[i]) + jax.nn.sigmoid(g_ssd) * (ssd @ w_out_ssd[i])
        x = x + mix @ w_out[i]
        h = rms_norm(x, norm_ffn[i])
        gt, up = jnp.split(h @ w_gate_up[i], 2, axis=-1)
        x = x + (jax.nn.silu(gt) * up) @ w_down[i]
    return rms_norm(x, final_norm)


def setup_inputs(seed: int = 0) -> dict:
    key = jax.random.key(seed)
    ks = jax.random.split(key, 20)
    f32 = jnp.float32
    res_scale = (2 * DEPTH) ** -0.5
    nrm = lambda k, shape, s: jax.random.normal(k, shape, f32) * s
    dt0 = jnp.exp(jax.random.uniform(ks[5], (DEPTH, 2, SSD_HEADS), f32, math.log(1e-3), math.log(1e-1)))
    return {
        "x_prompt": jax.random.normal(ks[0], (BATCH, SEQ, D_MODEL), f32),
        "x_sample": jax.random.normal(ks[1], (DEC_BATCH, DEC_SEQ, D_MODEL), f32),
        "w_in": nrm(ks[2], (DEPTH, D_MODEL, N_IN), D_MODEL ** -0.5),
        "conv_w": nrm(ks[3], (DEPTH, D_CONV, CONV_CH), D_CONV ** -0.5),
        "conv_b": nrm(ks[4], (DEPTH, CONV_CH), 0.01),
        "a_log": jnp.log(jax.random.uniform(ks[6], (DEPTH, 2, SSD_HEADS), f32, 1.0, 16.0)),
        "dt_bias": dt0 + jnp.log(-jnp.expm1(-dt0)),
        "d_skip": 1.0 + nrm(ks[7], (DEPTH, SSD_HEADS), 0.1),
        "ssd_norm": 1.0 + nrm(ks[8], (DEPTH, D_INNER), 0.02),
        "attn_sink": nrm(ks[9], (DEPTH, N_HEADS), 0.5),
        "w_out_attn": nrm(ks[10], (DEPTH, ATTN_W, D_MODEL), ATTN_W ** -0.5),
        "w_out_ssd": nrm(ks[11], (DEPTH, D_INNER, D_MODEL), D_INNER ** -0.5),
        "w_out": nrm(ks[12], (DEPTH, D_MODEL, D_MODEL), D_MODEL ** -0.5 * res_scale),
        "norm_mix": 1.0 + nrm(ks[13], (DEPTH, D_MODEL), 0.02),
        "norm_ffn": 1.0 + nrm(ks[14], (DEPTH, D_MODEL), 0.02),
        "w_gate_up": nrm(ks[15], (DEPTH, D_MODEL, 2 * D_FF), D_MODEL ** -0.5),
        "w_down": nrm(ks[16], (DEPTH, D_FF, D_MODEL), D_FF ** -0.5 * res_scale),
        "final_norm": 1.0 + nrm(ks[17], (D_MODEL,), 0.02),
    }


def reference(x_prompt, x_sample, w_in, conv_w, conv_b, a_log, dt_bias, d_skip, ssd_norm, attn_sink,
              w_out_attn, w_out_ssd, w_out, norm_mix, norm_ffn, w_gate_up, w_down, final_norm):
    y_prompt = trunk(x_prompt, w_in, conv_w, conv_b, a_log, dt_bias, d_skip, ssd_norm, attn_sink,
                     w_out_attn, w_out_ssd, w_out, norm_mix, norm_ffn, w_gate_up, w_down, final_norm)
    y_sample = trunk(x_sample, w_in, conv_w, conv_b, a_log, dt_bias, d_skip, ssd_norm, attn_sink,
                     w_out_attn, w_out_ssd, w_out, norm_mix, norm_ffn, w_gate_up, w_down, final_norm)
    return (y_prompt, y_sample)
```

```python
import functools

import jax
import jax.numpy as jnp
import numpy as np
from jax import lax
from jax.experimental import pallas as pl
from jax.experimental.pallas import tpu as pltpu

D_MODEL = 2048
DEPTH = 4
N_HEADS = 16
N_KV_HEADS = 4
HEAD_DIM = 128
Q_PER_KV = N_HEADS // N_KV_HEADS
ATTN_W = N_HEADS * HEAD_DIM
KV_W = N_KV_HEADS * HEAD_DIM
BLOCK = 128
ROPE_THETA = 10000.0
D_INNER = 2 * D_MODEL
SSD_HEAD_DIM = 64
SSD_HEADS = D_INNER // SSD_HEAD_DIM
SSD_GROUPS = 8
HEADS_PER_GROUP = SSD_HEADS // SSD_GROUPS
GROUP_W = D_INNER // SSD_GROUPS
D_STATE = 128
D_CONV = 5
CHUNK = 128
CONV_CH = D_INNER + 2 * SSD_GROUPS * D_STATE
D_FF = ((8 * D_MODEL // 3 + 255) // 256) * 256
EPS = 1e-6

OFF_Q = 0
OFF_K = OFF_Q + ATTN_W
OFF_V = OFF_K + KV_W
OFF_GATT = OFF_V + KV_W
OFF_GSSD = OFF_GATT + D_MODEL
OFF_Z = OFF_GSSD + D_MODEL
OFF_XBC = OFF_Z + D_INNER
OFF_DT = OFF_XBC + CONV_CH
N_IN = OFF_DT + 2 * SSD_HEADS
IN_TILE = 768
N_IN_PAD = ((N_IN + IN_TILE - 1) // IN_TILE) * IN_TILE

LANES = 128
SUBLANES = 8
VMEM_LIMIT = 56 * 1024 * 1024
NEG_BIG = -1e30

F32 = jnp.float32
BF16 = jnp.bfloat16


def _params(*sem):
    return pltpu.CompilerParams(dimension_semantics=sem, vmem_limit_bytes=VMEM_LIMIT)


def _sigmoid(x):
    return 1.0 / (1.0 + jnp.exp(-x))


def _dot(a, b):
    return jnp.dot(a, b, preferred_element_type=F32)


NORM_ROWS = 128


def _norm_into(x_ref, g_ref, h_ref):
    g = g_ref[...]

    def body(i, c):
        r = pl.ds(pl.multiple_of(i * NORM_ROWS, NORM_ROWS), NORM_ROWS)
        x = x_ref[r, :]
        ms = jnp.mean(x * x, axis=-1, keepdims=True)
        h_ref[r, :] = ((x * lax.rsqrt(ms + EPS)) * g).astype(BF16)
        return c

    lax.fori_loop(0, x_ref.shape[0] // NORM_ROWS, body, 0)


def _norm_matmul_kernel(x_ref, g_ref, w_ref, o_ref, h_ref):
    @pl.when(pl.program_id(1) == 0)
    def _():
        _norm_into(x_ref, g_ref, h_ref)

    o_ref[...] = _dot(h_ref[...], w_ref[...]).astype(o_ref.dtype)


def norm_matmul(x, g, w, *, tm, tn, out_dtype):
    t, k = x.shape
    n = w.shape[1]
    return pl.pallas_call(
        _norm_matmul_kernel,
        grid=(t // tm, n // tn),
        in_specs=[
            pl.BlockSpec((tm, k), lambda i, j: (i, 0)),
            pl.BlockSpec((1, k), lambda i, j: (0, 0)),
            pl.BlockSpec((k, tn), lambda i, j: (0, j)),
        ],
        out_specs=pl.BlockSpec((tm, tn), lambda i, j: (i, j)),
        out_shape=jax.ShapeDtypeStruct((t, n), out_dtype),
        scratch_shapes=[pltpu.VMEM((tm, k), BF16)],
        compiler_params=_params("parallel", "arbitrary"),
        name="norm_in_proj",
    )(x, g, w)


def _norm_swiglu_kernel(x_ref, g_ref, wg_ref, wu_ref, o_ref, h_ref):
    @pl.when(pl.program_id(1) == 0)
    def _():
        _norm_into(x_ref, g_ref, h_ref)

    h = h_ref[...]
    gt = _dot(h, wg_ref[...])
    up = _dot(h, wu_ref[...])
    o_ref[...] = ((gt * _sigmoid(gt)) * up).astype(o_ref.dtype)


def norm_swiglu(x, g, w_gate_up, *, tm, tn):
    t, k = x.shape
    f = w_gate_up.shape[1] // 2
    nf = f // tn
    return pl.pallas_call(
        _norm_swiglu_kernel,
        grid=(t // tm, nf),
        in_specs=[
            pl.BlockSpec((tm, k), lambda i, j: (i, 0)),
            pl.BlockSpec((1, k), lambda i, j: (0, 0)),
            pl.BlockSpec((k, tn), lambda i, j: (0, j)),
            pl.BlockSpec((k, tn), lambda i, j: (0, j + nf)),
        ],
        out_specs=pl.BlockSpec((tm, tn), lambda i, j: (i, j)),
        out_shape=jax.ShapeDtypeStruct((t, f), BF16),
        scratch_shapes=[pltpu.VMEM((tm, k), BF16)],
        compiler_params=_params("parallel", "arbitrary"),
        name="norm_swiglu",
    )(x, g, w_gate_up, w_gate_up)


def _resid_matmul_kernel(a_ref, w_ref, x_ref, o_ref):
    o_ref[...] = x_ref[...] + _dot(a_ref[...], w_ref[...])


def resid_matmul(a, w, x, *, tm, tn):
    t, k = a.shape
    n = w.shape[1]
    return pl.pallas_call(
        _resid_matmul_kernel,
        grid=(t // tm, n // tn),
        in_specs=[
            pl.BlockSpec((tm, k), lambda i, j: (i, 0)),
            pl.BlockSpec((k, tn), lambda i, j: (0, j)),
            pl.BlockSpec((tm, tn), lambda i, j: (i, j)),
        ],
        out_specs=pl.BlockSpec((tm, tn), lambda i, j: (i, j)),
        out_shape=jax.ShapeDtypeStruct((t, n), F32),
        compiler_params=_params("parallel", "arbitrary"),
        name="resid_matmul",
    )(a, w, x)


def _mix_kernel(attn_ref, ssd_ref, wa_ref, ws_ref, ga_ref, gs_ref, o_ref):
    a = _dot(attn_ref[...], wa_ref[...])
    s = _dot(ssd_ref[...], ws_ref[...])
    o_ref[...] = (_sigmoid(ga_ref[...]) * a + _sigmoid(gs_ref[...]) * s).astype(o_ref.dtype)


def gated_mix(attn, ssd, wa, ws, proj, *, tm, tn):
    t = attn.shape[0]
    n = wa.shape[1]
    return pl.pallas_call(
        _mix_kernel,
        grid=(t // tm, n // tn),
        in_specs=[
            pl.BlockSpec((tm, attn.shape[1]), lambda i, j: (i, 0)),
            pl.BlockSpec((tm, ssd.shape[1]), lambda i, j: (i, 0)),
            pl.BlockSpec((wa.shape[0], tn), lambda i, j: (0, j)),
            pl.BlockSpec((ws.shape[0], tn), lambda i, j: (0, j)),
            pl.BlockSpec((tm, tn), lambda i, j: (i, j + OFF_GATT // tn)),
            pl.BlockSpec((tm, tn), lambda i, j: (i, j + OFF_GSSD // tn)),
        ],
        out_specs=pl.BlockSpec((tm, tn), lambda i, j: (i, j)),
        out_shape=jax.ShapeDtypeStruct((t, n), BF16),
        compiler_params=_params("parallel", "arbitrary"),
        name="gated_mix",
    )(attn, ssd, wa, ws, proj, proj)


def _rmsnorm_kernel(x_ref, g_ref, o_ref):
    x = x_ref[...]
    ms = jnp.mean(x * x, axis=-1, keepdims=True)
    o_ref[...] = (x * lax.rsqrt(ms + EPS)) * g_ref[...]


def rmsnorm(x, g, *, tm):
    t, k = x.shape
    return pl.pallas_call(
        _rmsnorm_kernel,
        grid=(t // tm,),
        in_specs=[pl.BlockSpec((tm, k), lambda i: (i, 0)), pl.BlockSpec((1, k), lambda i: (0, 0))],
        out_specs=pl.BlockSpec((tm, k), lambda i: (i, 0)),
        out_shape=jax.ShapeDtypeStruct((t, k), F32),
        compiler_params=_params("parallel"),
        name="final_rmsnorm",
    )(x, g)


def _rope(x, cc, ss):
    return x * cc + pltpu.roll(x, HEAD_DIM // 2, 1) * ss


def _attn_kernel(sink_ref, q_ref, kp_ref, kc_ref, kn_ref, vp_ref, vc_ref, vn_ref, cc_ref, ss_ref, o_ref):
    n = pl.program_id(1)
    nb = pl.num_programs(1)
    pos_c = pl.multiple_of(n * BLOCK, BLOCK)
    pos_p = pl.multiple_of(jnp.maximum(n - 1, 0) * BLOCK, BLOCK)
    pos_n = pl.multiple_of(jnp.minimum(n + 1, nb - 1) * BLOCK, BLOCK)
    cc_c, ss_c = cc_ref[pl.ds(pos_c, BLOCK), :], ss_ref[pl.ds(pos_c, BLOCK), :]
    cc_k = jnp.concatenate([cc_ref[pl.ds(pos_p, BLOCK), :], cc_c, cc_ref[pl.ds(pos_n, BLOCK), :]], axis=0)
    ss_k = jnp.concatenate([ss_ref[pl.ds(pos_p, BLOCK), :], ss_c, ss_ref[pl.ds(pos_n, BLOCK), :]], axis=0)
    cc_q = jnp.concatenate([cc_c] * Q_PER_KV, axis=0)
    ss_q = jnp.concatenate([ss_c] * Q_PER_KV, axis=0)

    rows = Q_PER_KV * BLOCK
    t = lax.broadcasted_iota(jnp.int32, (rows, 3 * BLOCK), 0) % BLOCK
    sk = lax.broadcasted_iota(jnp.int32, (rows, 3 * BLOCK), 1)
    has_prev = (n > 0).astype(jnp.int32)
    has_next = (n < nb - 1).astype(jnp.int32)
    lo = t * has_prev + BLOCK * (1 - has_prev)
    hi = (t + 2 * BLOCK) * has_next + (2 * BLOCK - 1) * (1 - has_next)
    valid = (sk >= lo) & (sk <= hi)
    scale = HEAD_DIM ** -0.5

    for hk in range(N_KV_HEADS):
        c = slice(hk * HEAD_DIM, (hk + 1) * HEAD_DIM)
        kwin = jnp.concatenate([kp_ref[0, :, c], kc_ref[0, :, c], kn_ref[0, :, c]], axis=0)
        vwin = jnp.concatenate([vp_ref[0, :, c], vc_ref[0, :, c], vn_ref[0, :, c]], axis=0).astype(BF16)
        kr = _rope(kwin, cc_k, ss_k).astype(BF16)
        qg = jnp.concatenate(
            [q_ref[0, :, (hk * Q_PER_KV + g) * HEAD_DIM:(hk * Q_PER_KV + g + 1) * HEAD_DIM] for g in range(Q_PER_KV)],
            axis=0,
        )
        qr = _rope(qg, cc_q, ss_q).astype(BF16)
        s = lax.dot_general(qr, kr, (((1,), (1,)), ((), ())), preferred_element_type=F32) * scale
        s = jnp.where(valid, s, NEG_BIG)
        ps, dens = [], []
        for g in range(Q_PER_KV):
            sg = s[g * BLOCK:(g + 1) * BLOCK]
            sink = sink_ref[hk * Q_PER_KV + g]
            m = jnp.maximum(jnp.max(sg, axis=-1, keepdims=True), sink)
            p = jnp.exp(sg - m)
            dens.append(jnp.sum(p, axis=-1, keepdims=True) + jnp.exp(sink - m))
            ps.append(p.astype(BF16))
        pv = _dot(jnp.concatenate(ps, axis=0), vwin)
        for g in range(Q_PER_KV):
            h = hk * Q_PER_KV + g
            o = pv[g * BLOCK:(g + 1) * BLOCK] / dens[g]
            o_ref[0, :, h * HEAD_DIM:(h + 1) * HEAD_DIM] = o.astype(o_ref.dtype)


def windowed_attention(proj3, sink, cc, ss):
    nseq, l, _ = proj3.shape
    nb = l // BLOCK
    kcol, vcol = OFF_K // KV_W, OFF_V // KV_W
    prev = lambda n: jnp.maximum(n - 1, 0)
    nxt = lambda n: jnp.minimum(n + 1, nb - 1)
    kv = lambda rowf, col: pl.BlockSpec((1, BLOCK, KV_W), lambda b, n: (b, rowf(n), col))
    ident = lambda n: n
    return pl.pallas_call(
        _attn_kernel,
        grid=(nseq, nb),
        in_specs=[
            pl.BlockSpec(memory_space=pltpu.SMEM),
            pl.BlockSpec((1, BLOCK, ATTN_W), lambda b, n: (b, n, OFF_Q // ATTN_W)),
            kv(prev, kcol), kv(ident, kcol), kv(nxt, kcol),
            kv(prev, vcol), kv(ident, vcol), kv(nxt, vcol),
            pl.BlockSpec((l, HEAD_DIM), lambda b, n: (0, 0)),
            pl.BlockSpec((l, HEAD_DIM), lambda b, n: (0, 0)),
        ],
        out_specs=pl.BlockSpec((1, BLOCK, ATTN_W), lambda b, n: (b, n, 0)),
        out_shape=jax.ShapeDtypeStruct((nseq, l, ATTN_W), BF16),
        compiler_params=_params("parallel", "arbitrary"),
        name="windowed_attention",
    )(sink, proj3, proj3, proj3, proj3, proj3, proj3, proj3, cc, ss)


CONV_ROWS = 512
CONV_COLS = 512
HALO = SUBLANES
PAD = D_CONV // 2


def _conv_kernel(xp_ref, xc_ref, xn_ref, w_ref, b_ref, o_ref, ext_ref):
    i = pl.program_id(1)
    ni = pl.num_programs(1)
    rows = xc_ref.shape[1]
    ext_ref[0:HALO, :] = jnp.where(i > 0, xp_ref[0], 0.0)
    ext_ref[HALO:HALO + rows, :] = xc_ref[0]
    ext_ref[HALO + rows:, :] = jnp.where(i < ni - 1, xn_ref[0], 0.0)
    acc = ext_ref[HALO - PAD:HALO - PAD + rows, :] * w_ref[0:1, :]
    for j in range(1, D_CONV):
        acc = acc + ext_ref[HALO - PAD + j:HALO - PAD + j + rows, :] * w_ref[j:j + 1, :]
    acc = acc + b_ref[...]
    o_ref[0] = (acc * _sigmoid(acc)).astype(o_ref.dtype)


def conv_silu(proj3, conv_w, conv_b, *, ch_off, n_ch, out_dtype):
    nseq, l, _ = proj3.shape
    rows = min(CONV_ROWS, l)
    col0 = (OFF_XBC + ch_off) // CONV_COLS
    wcol0 = ch_off // CONV_COLS
    hb = rows // HALO
    nhalo = l // HALO
    return pl.pallas_call(
        _conv_kernel,
        grid=(nseq, l // rows, n_ch // CONV_COLS),
        in_specs=[
            pl.BlockSpec((1, HALO, CONV_COLS), lambda b, i, c: (b, jnp.maximum(i * hb - 1, 0), col0 + c)),
            pl.BlockSpec((1, rows, CONV_COLS), lambda b, i, c: (b, i, col0 + c)),
            pl.BlockSpec((1, HALO, CONV_COLS), lambda b, i, c: (b, jnp.minimum((i + 1) * hb, nhalo - 1), col0 + c)),
            pl.BlockSpec((D_CONV, CONV_COLS), lambda b, i, c: (0, wcol0 + c)),
            pl.BlockSpec((1, CONV_COLS), lambda b, i, c: (0, wcol0 + c)),
        ],
        out_specs=pl.BlockSpec((1, rows, CONV_COLS), lambda b, i, c: (b, i, c)),
        out_shape=jax.ShapeDtypeStruct((nseq, l, n_ch), out_dtype),
        scratch_shapes=[pltpu.VMEM((rows + 2 * HALO, CONV_COLS), F32)],
        compiler_params=_params("parallel", "parallel", "arbitrary"),
        name="conv_silu",
    )(proj3, proj3, proj3, conv_w, conv_b)


DT_CHUNKS = 8


def _split3(a):
    hi = a.astype(BF16)
    r = a - hi.astype(F32)
    mid = r.astype(BF16)
    lo = (r - mid.astype(F32)).astype(BF16)
    return hi, mid, lo


def _tri_sum(tri, a):
    hi, mid, lo = _split3(a)
    return _dot(tri, hi) + _dot(tri, mid) + _dot(tri, lo)


def _dt_kernel(raw_ref, bias_ref, alog_ref, acum_ref, dt_ref, acum_t_ref, dt_t_ref, tot_ref):
    neg_a = -jnp.exp(alog_ref[...])
    bias = bias_ref[...]
    r = lax.broadcasted_iota(jnp.int32, (CHUNK, CHUNK), 0)
    c = lax.broadcasted_iota(jnp.int32, (CHUNK, CHUNK), 1)
    lower = (r >= c).astype(BF16)
    upper = (r <= c).astype(BF16)
    fwd_lane = lax.broadcasted_iota(jnp.int32, (CHUNK, 2 * SSD_HEADS), 1) < SSD_HEADS
    for ci in range(raw_ref.shape[0] // CHUNK):
        rows = slice(ci * CHUNK, (ci + 1) * CHUNK)
        raw = raw_ref[rows, :] + bias
        dt = jnp.maximum(raw, 0.0) + jnp.log1p(jnp.exp(-jnp.abs(raw)))
        a = dt * neg_a
        pre = _tri_sum(lower, a)
        suf = _tri_sum(upper, a)
        acum = jnp.where(fwd_lane, pre, suf)
        acum_ref[rows, :] = acum
        dt_ref[rows, :] = dt
        acum_t_ref[ci] = acum.T
        dt_t_ref[ci] = dt.T
        tot_ref[ci] = jnp.where(fwd_lane[0:1], pre[CHUNK - 1:CHUNK], suf[0:1])


def dt_prepare(proj, dt_bias, a_log):
    t = proj.shape[0]
    rows = DT_CHUNKS * CHUNK
    if t % rows:
        rows = CHUNK
    cps = rows // CHUNK
    nchunks = t // CHUNK
    w = 2 * SSD_HEADS
    col = OFF_DT // w
    return pl.pallas_call(
        _dt_kernel,
        grid=(t // rows,),
        in_specs=[
            pl.BlockSpec((rows, w), lambda i: (i, col)),
            pl.BlockSpec((1, w), lambda i: (0, 0)),
            pl.BlockSpec((1, w), lambda i: (0, 0)),
        ],
        out_specs=[
            pl.BlockSpec((rows, w), lambda i: (i, 0)),
            pl.BlockSpec((rows, w), lambda i: (i, 0)),
            pl.BlockSpec((cps, w, CHUNK), lambda i: (i, 0, 0)),
            pl.BlockSpec((cps, w, CHUNK), lambda i: (i, 0, 0)),
            pl.BlockSpec((cps, 1, w), lambda i: (i, 0, 0)),
        ],
        out_shape=[
            jax.ShapeDtypeStruct((t, w), F32),
            jax.ShapeDtypeStruct((t, w), F32),
            jax.ShapeDtypeStruct((nchunks, w, CHUNK), F32),
            jax.ShapeDtypeStruct((nchunks, w, CHUNK), F32),
            jax.ShapeDtypeStruct((nchunks, 1, w), F32),
        ],
        compiler_params=_params("parallel"),
        name="dt_prepare",
    )(proj, dt_bias, a_log)


SSD_CPS = 4
PAIRS = HEADS_PER_GROUP // 2
COL_ACUM = 0
COL_DT = 2 * HEADS_PER_GROUP


def _expand_pairs(cols, lane_lo):
    return jnp.concatenate(
        [jnp.where(lane_lo, cols[2 * p], cols[2 * p + 1]) for p in range(PAIRS)], axis=1)


def _ssd_kernel(x_ref, b_ref, c_ref, z_ref, col_ref, rowf_a_ref, rowf_d_ref, rowb_a_ref, rowb_d_ref,
                tot_ref, skip_ref, ng_ref, o_ref, sf_ref, sb_ref, sb_all_ref):
    ph = pl.program_id(2)
    cs = pl.program_id(3)
    ncs = pl.num_programs(3)
    cps = x_ref.shape[1] // CHUNK
    lane_lo = lax.broadcasted_iota(jnp.int32, (CHUNK, LANES), 1) < SSD_HEAD_DIM

    def bcast(colv, lane):
        return jnp.broadcast_to(colv[:, lane:lane + 1], (CHUNK, LANES))

    def state_update(s_ref, xs, bc, acum_x, dt_x, tot_row):
        w = dt_x * jnp.exp(tot_row - acum_x)
        xw = (xs * w).astype(BF16)
        st = lax.dot_general(bc, xw, (((0,), (0,)), ((), ())), preferred_element_type=F32)
        s_ref[...] = s_ref[...] * jnp.exp(tot_row) + st

    @pl.when(ph == 0)
    def _backward_states():
        @pl.when(cs == 0)
        def _():
            sb_ref[...] = jnp.zeros_like(sb_ref)

        for k in range(cps):
            ci = cps - 1 - k
            rows = slice(ci * CHUNK, (ci + 1) * CHUNK)
            chunk = (ncs - 1 - cs) * cps + ci
            sb_all_ref[chunk] = sb_ref[...].astype(BF16)
            colv = col_ref[0, 0, rows, :]
            ab = [bcast(colv, COL_ACUM + HEADS_PER_GROUP + j) for j in range(HEADS_PER_GROUP)]
            db = [bcast(colv, COL_DT + HEADS_PER_GROUP + j) for j in range(HEADS_PER_GROUP)]
            state_update(sb_ref, x_ref[0, rows, :], b_ref[0, rows, :],
                         _expand_pairs(ab, lane_lo), _expand_pairs(db, lane_lo), tot_ref[0, ci, 1])

    @pl.when(ph == 1)
    def _forward_and_output():
        @pl.when(cs == 0)
        def _():
            sf_ref[...] = jnp.zeros_like(sf_ref)

        li = lax.broadcasted_iota(jnp.int32, (CHUNK, CHUNK), 0)
        si = lax.broadcasted_iota(jnp.int32, (CHUNK, CHUNK), 1)
        tril = li >= si
        triu = li <= si
        for ci in range(cps):
            rows = slice(ci * CHUNK, (ci + 1) * CHUNK)
            chunk = cs * cps + ci
            xs = x_ref[0, rows, :]
            xs16 = xs.astype(BF16)
            bc = b_ref[0, rows, :]
            cc = c_ref[0, rows, :]
            cb = lax.dot_general(cc, bc, (((1,), (1,)), ((), ())), preferred_element_type=F32)
            colv = col_ref[0, 0, rows, :]
            af = [bcast(colv, COL_ACUM + j) for j in range(HEADS_PER_GROUP)]
            ab = [bcast(colv, COL_ACUM + HEADS_PER_GROUP + j) for j in range(HEADS_PER_GROUP)]
            df = [bcast(colv, COL_DT + j) for j in range(HEADS_PER_GROUP)]
            af_x, ab_x = _expand_pairs(af, lane_lo), _expand_pairs(ab, lane_lo)

            ydiag = []
            for p in range(PAIRS):
                ms = []
                for j in (2 * p, 2 * p + 1):
                    seg_f = af[j] - rowf_a_ref[0, ci, j:j + 1, :]
                    seg_b = ab[j] - rowb_a_ref[0, ci, j:j + 1, :]
                    m = (jnp.exp(jnp.where(tril, seg_f, NEG_BIG)) * rowf_d_ref[0, ci, j:j + 1, :]
                         + jnp.exp(jnp.where(triu, seg_b, NEG_BIG)) * rowb_d_ref[0, ci, j:j + 1, :])
                    ms.append((cb * m).astype(BF16))
                xp = xs16[:, p * LANES:(p + 1) * LANES]
                zero = jnp.zeros_like(xp)
                xpair = jnp.concatenate([jnp.where(lane_lo, xp, zero), jnp.where(lane_lo, zero, xp)], axis=0)
                ydiag.append(_dot(jnp.concatenate(ms, axis=1), xpair))
            y = jnp.concatenate(ydiag, axis=1)

            y = y + _dot(cc, sf_ref[...].astype(BF16)) * jnp.exp(af_x)
            y = y + _dot(cc, sb_all_ref[chunk]) * jnp.exp(ab_x)
            y = y + xs * skip_ref[...]

            state_update(sf_ref, xs, bc, af_x, _expand_pairs(df, lane_lo), tot_ref[0, ci, 0])

            z = z_ref[0, rows, :]
            y = y * (z * _sigmoid(z))
            ms_ = jnp.mean(y * y, axis=-1, keepdims=True)
            o_ref[0, rows, :] = ((y * lax.rsqrt(ms_ + EPS)) * ng_ref[...]).astype(o_ref.dtype)


def ssd_scan(xs, bc, proj3, colpack, acum_t, dt_t, tot_x, skip_x, norm_g):
    nseq, l, _ = xs.shape
    nchunks = l // CHUNK
    cps = SSD_CPS if nchunks % SSD_CPS == 0 else 1
    ncs = nchunks // cps
    rows = cps * CHUNK
    blk = lambda ph, cs: jnp.where(ph == 0, ncs - 1 - cs, cs)
    fwd_only = lambda ph, cs: jnp.where(ph == 0, 0, cs)
    zcol = OFF_Z // GROUP_W
    row_spec = lambda d: pl.BlockSpec(
        (1, cps, HEADS_PER_GROUP, CHUNK), lambda b, g, ph, cs: (b, blk(ph, cs), d * SSD_GROUPS + g, 0))
    return pl.pallas_call(
        _ssd_kernel,
        grid=(nseq, SSD_GROUPS, 2, ncs),
        in_specs=[
            pl.BlockSpec((1, rows, GROUP_W), lambda b, g, ph, cs: (b, blk(ph, cs), g)),
            pl.BlockSpec((1, rows, D_STATE), lambda b, g, ph, cs: (b, blk(ph, cs), g)),
            pl.BlockSpec((1, rows, D_STATE), lambda b, g, ph, cs: (b, fwd_only(ph, cs), SSD_GROUPS + g)),
            pl.BlockSpec((1, rows, GROUP_W), lambda b, g, ph, cs: (b, fwd_only(ph, cs), zcol + g)),
            pl.BlockSpec((1, 1, rows, 4 * HEADS_PER_GROUP), lambda b, g, ph, cs: (b, g, blk(ph, cs), 0)),
            row_spec(0), row_spec(0), row_spec(1), row_spec(1),
            pl.BlockSpec((1, cps, 2, 1, GROUP_W), lambda b, g, ph, cs: (b, blk(ph, cs), 0, 0, g)),
            pl.BlockSpec((1, GROUP_W), lambda b, g, ph, cs: (0, g)),
            pl.BlockSpec((1, GROUP_W), lambda b, g, ph, cs: (0, g)),
        ],
        out_specs=pl.BlockSpec((1, rows, GROUP_W), lambda b, g, ph, cs: (b, fwd_only(ph, cs), g)),
        out_shape=jax.ShapeDtypeStruct((nseq, l, D_INNER), BF16),
        scratch_shapes=[
            pltpu.VMEM((D_STATE, GROUP_W), F32),
            pltpu.VMEM((D_STATE, GROUP_W), F32),
            pltpu.VMEM((nchunks, D_STATE, GROUP_W), BF16),
        ],
        compiler_params=_params("parallel", "parallel", "arbitrary", "arbitrary"),
        name="ssd_scan",
    )(xs, bc, bc, proj3, colpack, acum_t, dt_t, acum_t, dt_t, tot_x, skip_x, norm_g)


def _rope_tables(l):
    half = HEAD_DIM // 2
    inv_freq = ROPE_THETA ** (-jnp.arange(half, dtype=F32) * 2.0 / HEAD_DIM)
    ang = jnp.arange(l, dtype=F32)[:, None] * inv_freq[None, :]
    cos, sin = jnp.cos(ang), jnp.sin(ang)
    return jnp.concatenate([cos, cos], axis=-1), jnp.concatenate([-sin, sin], axis=-1)


def _row_tile(t, want):
    while t % want:
        want //= 2
    return want


def _layer(x, nseq, l, cc, ss, w_in, conv_w, conv_b, a_log, dt_bias, d_skip, ssd_norm, attn_sink,
           w_out_attn, w_out_ssd, w_out, norm_mix, norm_ffn, w_gate_up, w_down):
    t = nseq * l
    nchunks = l // CHUNK
    proj = norm_matmul(x, norm_mix, w_in, tm=_row_tile(t, 1024), tn=IN_TILE, out_dtype=F32)
    proj3 = proj.reshape(nseq, l, N_IN_PAD)

    attn = windowed_attention(proj3, attn_sink, cc, ss).reshape(t, ATTN_W)

    xs = conv_silu(proj3, conv_w, conv_b, ch_off=0, n_ch=D_INNER, out_dtype=F32)
    bcm = conv_silu(proj3, conv_w, conv_b, ch_off=D_INNER, n_ch=CONV_CH - D_INNER, out_dtype=BF16)

    acum, dtv, acum_t, dt_t, tot = dt_prepare(proj, dt_bias, a_log)
    colpack = jnp.stack([acum, dtv], axis=1).reshape(nseq, l, 2, 2, SSD_GROUPS, HEADS_PER_GROUP)
    colpack = colpack.transpose(0, 4, 1, 2, 3, 5).reshape(nseq, SSD_GROUPS, l, 4 * HEADS_PER_GROUP)
    tot_x = jnp.repeat(tot.reshape(nseq, nchunks, 2, SSD_HEADS), SSD_HEAD_DIM, axis=-1)
    tot_x = tot_x.reshape(nseq, nchunks, 2, 1, D_INNER)
    acum_t = acum_t.reshape(nseq, nchunks, 2 * SSD_HEADS, CHUNK)
    dt_t = dt_t.reshape(nseq, nchunks, 2 * SSD_HEADS, CHUNK)
    skip_x = jnp.repeat(d_skip, SSD_HEAD_DIM)[None, :]
    ssd = ssd_scan(xs, bcm, proj3, colpack, acum_t, dt_t, tot_x, skip_x, ssd_norm).reshape(t, D_INNER)

    mix = gated_mix(attn, ssd, w_out_attn, w_out_ssd, proj, tm=_row_tile(t, 512), tn=512)
    x = resid_matmul(mix, w_out, x, tm=_row_tile(t, 1024), tn=512)
    act = norm_swiglu(x, norm_ffn, w_gate_up, tm=_row_tile(t, 1024), tn=512)
    x = resid_matmul(act, w_down, x, tm=_row_tile(t, 1024), tn=512)
    return x


def _trunk(x, w_in, conv_w, conv_b, a_log, dt_bias, d_skip, ssd_norm, attn_sink,
           w_out_attn, w_out_ssd, w_out, norm_mix, norm_ffn, w_gate_up, w_down, final_norm):
    nseq, l, _ = x.shape
    t = nseq * l
    x = x.reshape(t, D_MODEL)
    cc, ss = _rope_tables(l)
    w_in16 = jnp.pad(w_in, ((0, 0), (0, 0), (0, N_IN_PAD - N_IN))).astype(BF16)
    for i in range(DEPTH):
        x = _layer(
            x, nseq, l, cc, ss, w_in16[i], conv_w[i], conv_b[i][None, :],
            a_log[i].reshape(1, -1), dt_bias[i].reshape(1, -1), d_skip[i], ssd_norm[i][None, :], attn_sink[i],
            w_out_attn[i].astype(BF16), w_out_ssd[i].astype(BF16), w_out[i].astype(BF16),
            norm_mix[i][None, :], norm_ffn[i][None, :], w_gate_up[i].astype(BF16), w_down[i].astype(BF16))
    y = rmsnorm(x, final_norm[None, :], tm=_row_tile(t, 512))
    return y.reshape(nseq, l, D_MODEL)


def kernel(x_prompt, x_sample, w_in, conv_w, conv_b, a_log, dt_bias, d_skip, ssd_norm, attn_sink,
           w_out_attn, w_out_ssd, w_out, norm_mix, norm_ffn, w_gate_up, w_down, final_norm):
    nb = x_prompt.shape[0]
    x = jnp.concatenate([x_prompt, x_sample], axis=0)
    y = _trunk(x, w_in, conv_w, conv_b, a_log, dt_bias, d_skip, ssd_norm, attn_sink,
               w_out_attn, w_out_ssd, w_out, norm_mix, norm_ffn, w_gate_up, w_down, final_norm)
    return (y[:nb], y[nb:])
```

```python
import jax
import jax.numpy as jnp
from jax import lax
from jax.experimental import pallas as pl
from jax.experimental.pallas import tpu as pltpu

D_MODEL = 2048
DEPTH = 4
N_HEADS = 16
N_KV_HEADS = 4
HEAD_DIM = 128
Q_PER_KV = N_HEADS // N_KV_HEADS
ATTN_W = N_HEADS * HEAD_DIM
KV_W = N_KV_HEADS * HEAD_DIM
BLOCK = 128
ROPE_THETA = 10000.0
D_INNER = 2 * D_MODEL
SSD_HEAD_DIM = 64
SSD_HEADS = D_INNER // SSD_HEAD_DIM
SSD_GROUPS = 8
HEADS_PER_GROUP = SSD_HEADS // SSD_GROUPS
GROUP_W = D_INNER // SSD_GROUPS
D_STATE = 128
D_CONV = 5
CHUNK = 128
CONV_CH = D_INNER + 2 * SSD_GROUPS * D_STATE
D_FF = ((8 * D_MODEL // 3 + 255) // 256) * 256
EPS = 1e-6

OFF_Q = 0
OFF_K = OFF_Q + ATTN_W
OFF_V = OFF_K + KV_W
OFF_GATT = OFF_V + KV_W
OFF_GSSD = OFF_GATT + D_MODEL
OFF_Z = OFF_GSSD + D_MODEL
OFF_XBC = OFF_Z + D_INNER
OFF_DT = OFF_XBC + CONV_CH
N_IN = OFF_DT + 2 * SSD_HEADS
IN_TILE = 768
N_IN_PAD = ((N_IN + IN_TILE - 1) // IN_TILE) * IN_TILE

LANES = 128
SUBLANES = 8
VMEM_LIMIT = 56 * 1024 * 1024
NEG_BIG = -1e30

F32 = jnp.float32
BF16 = jnp.bfloat16


def _params(*sem):
    return pltpu.CompilerParams(dimension_semantics=sem, vmem_limit_bytes=VMEM_LIMIT)


def _sigmoid(x):
    return 1.0 / (1.0 + jnp.exp(-x))


def _dot(a, b):
    return jnp.dot(a, b, preferred_element_type=F32)


NORM_ROWS = 128


def _norm_into(x_ref, g_ref, h_ref):
    g = g_ref[...]

    def body(i, c):
        r = pl.ds(pl.multiple_of(i * NORM_ROWS, NORM_ROWS), NORM_ROWS)
        x = x_ref[r, :]
        ms = jnp.mean(x * x, axis=-1, keepdims=True)
        h_ref[r, :] = ((x * lax.rsqrt(ms + EPS)) * g).astype(BF16)
        return c

    lax.fori_loop(0, x_ref.shape[0] // NORM_ROWS, body, 0)


def _norm_matmul_kernel(x_ref, g_ref, w_ref, o_ref, h_ref):
    @pl.when(pl.program_id(1) == 0)
    def _():
        _norm_into(x_ref, g_ref, h_ref)

    o_ref[...] = _dot(h_ref[...], w_ref[...]).astype(o_ref.dtype)


def norm_matmul(x, g, w, *, tm, tn, out_dtype):
    t, k = x.shape
    n = w.shape[1]
    return pl.pallas_call(
        _norm_matmul_kernel,
        grid=(t // tm, n // tn),
        in_specs=[
            pl.BlockSpec((tm, k), lambda i, j: (i, 0)),
            pl.BlockSpec((1, k), lambda i, j: (0, 0)),
            pl.BlockSpec((k, tn), lambda i, j: (0, j)),
        ],
        out_specs=pl.BlockSpec((tm, tn), lambda i, j: (i, j)),
        out_shape=jax.ShapeDtypeStruct((t, n), out_dtype),
        scratch_shapes=[pltpu.VMEM((tm, k), BF16)],
        compiler_params=_params("parallel", "arbitrary"),
        name="norm_in_proj",
    )(x, g, w)


def _norm_swiglu_kernel(x_ref, g_ref, wg_ref, wu_ref, o_ref, h_ref):
    @pl.when(pl.program_id(1) == 0)
    def _():
        _norm_into(x_ref, g_ref, h_ref)

    h = h_ref[...]
    gt = _dot(h, wg_ref[...])
    up = _dot(h, wu_ref[...])
    o_ref[...] = ((gt * _sigmoid(gt)) * up).astype(o_ref.dtype)


def norm_swiglu(x, g, w_gate_up, *, tm, tn):
    t, k = x.shape
    f = w_gate_up.shape[1] // 2
    nf = f // tn
    return pl.pallas_call(
        _norm_swiglu_kernel,
        grid=(t // tm, nf),
        in_specs=[
            pl.BlockSpec((tm, k), lambda i, j: (i, 0)),
            pl.BlockSpec((1, k), lambda i, j: (0, 0)),
            pl.BlockSpec((k, tn), lambda i, j: (0, j)),
            pl.BlockSpec((k, tn), lambda i, j: (0, j + nf)),
        ],
        out_specs=pl.BlockSpec((tm, tn), lambda i, j: (i, j)),
        out_shape=jax.ShapeDtypeStruct((t, f), BF16),
        scratch_shapes=[pltpu.VMEM((tm, k), BF16)],
        compiler_params=_params("parallel", "arbitrary"),
        name="norm_swiglu",
    )(x, g, w_gate_up, w_gate_up)


def _resid_matmul_kernel(a_ref, w_ref, x_ref, o_ref):
    o_ref[...] = x_ref[...] + _dot(a_ref[...], w_ref[...])


def resid_matmul(a, w, x, *, tm, tn):
    t, k = a.shape
    n = w.shape[1]
    return pl.pallas_call(
        _resid_matmul_kernel,
        grid=(t // tm, n // tn),
        in_specs=[
            pl.BlockSpec((tm, k), lambda i, j: (i, 0)),
            pl.BlockSpec((k, tn), lambda i, j: (0, j)),
            pl.BlockSpec((tm, tn), lambda i, j: (i, j)),
        ],
        out_specs=pl.BlockSpec((tm, tn), lambda i, j: (i, j)),
        out_shape=jax.ShapeDtypeStruct((t, n), F32),
        compiler_params=_params("parallel", "arbitrary"),
        name="resid_matmul",
    )(a, w, x)


def _mix_kernel(attn_ref, ssd_ref, wa_ref, ws_ref, ga_ref, gs_ref, o_ref):
    a = _dot(attn_ref[...], wa_ref[...])
    s = _dot(ssd_ref[...], ws_ref[...])
    o_ref[...] = (_sigmoid(ga_ref[...]) * a + _sigmoid(gs_ref[...]) * s).astype(o_ref.dtype)


def gated_mix(attn, ssd, wa, ws, proj, *, tm, tn):
    t = attn.shape[0]
    n = wa.shape[1]
    return pl.pallas_call(
        _mix_kernel,
        grid=(t // tm, n // tn),
        in_specs=[
            pl.BlockSpec((tm, attn.shape[1]), lambda i, j: (i, 0)),
            pl.BlockSpec((tm, ssd.shape[1]), lambda i, j: (i, 0)),
            pl.BlockSpec((wa.shape[0], tn), lambda i, j: (0, j)),
            pl.BlockSpec((ws.shape[0], tn), lambda i, j: (0, j)),
            pl.BlockSpec((tm, tn), lambda i, j: (i, j + OFF_GATT // tn)),
            pl.BlockSpec((tm, tn), lambda i, j: (i, j + OFF_GSSD // tn)),
        ],
        out_specs=pl.BlockSpec((tm, tn), lambda i, j: (i, j)),
        out_shape=jax.ShapeDtypeStruct((t, n), BF16),
        compiler_params=_params("parallel", "arbitrary"),
        name="gated_mix",
    )(attn, ssd, wa, ws, proj, proj)


def _rmsnorm_kernel(x_ref, g_ref, o_ref):
    x = x_ref[...]
    ms = jnp.mean(x * x, axis=-1, keepdims=True)
    o_ref[...] = (x * lax.rsqrt(ms + EPS)) * g_ref[...]


def rmsnorm(x, g, *, tm, row0, nrows):
    k = x.shape[1]
    off = row0 // tm
    return pl.pallas_call(
        _rmsnorm_kernel,
        grid=(nrows // tm,),
        in_specs=[pl.BlockSpec((tm, k), lambda i: (i + off, 0)), pl.BlockSpec((1, k), lambda i: (0, 0))],
        out_specs=pl.BlockSpec((tm, k), lambda i: (i, 0)),
        out_shape=jax.ShapeDtypeStruct((nrows, k), F32),
        compiler_params=_params("parallel"),
        name="final_rmsnorm",
    )(x, g)


def _rope(x, cc, ss):
    return x * cc + pltpu.roll(x, HEAD_DIM // 2, 1) * ss


def _attn_kernel(sink_ref, q_ref, kp_ref, kc_ref, kn_ref, vp_ref, vc_ref, vn_ref, cc_ref, ss_ref, o_ref):
    n = pl.program_id(1)
    nb = pl.num_programs(1)
    pos_c = pl.multiple_of(n * BLOCK, BLOCK)
    pos_p = pl.multiple_of(jnp.maximum(n - 1, 0) * BLOCK, BLOCK)
    pos_n = pl.multiple_of(jnp.minimum(n + 1, nb - 1) * BLOCK, BLOCK)
    cc_c, ss_c = cc_ref[pl.ds(pos_c, BLOCK), :], ss_ref[pl.ds(pos_c, BLOCK), :]
    cc_k = jnp.concatenate([cc_ref[pl.ds(pos_p, BLOCK), :], cc_c, cc_ref[pl.ds(pos_n, BLOCK), :]], axis=0)
    ss_k = jnp.concatenate([ss_ref[pl.ds(pos_p, BLOCK), :], ss_c, ss_ref[pl.ds(pos_n, BLOCK), :]], axis=0)
    cc_q = jnp.concatenate([cc_c] * Q_PER_KV, axis=0)
    ss_q = jnp.concatenate([ss_c] * Q_PER_KV, axis=0)

    rows = Q_PER_KV * BLOCK
    t = lax.broadcasted_iota(jnp.int32, (rows, 3 * BLOCK), 0) % BLOCK
    sk = lax.broadcasted_iota(jnp.int32, (rows, 3 * BLOCK), 1)
    has_prev = (n > 0).astype(jnp.int32)
    has_next = (n < nb - 1).astype(jnp.int32)
    lo = t * has_prev + BLOCK * (1 - has_prev)
    hi = (t + 2 * BLOCK) * has_next + (2 * BLOCK - 1) * (1 - has_next)
    valid = (sk >= lo) & (sk <= hi)
    scale = HEAD_DIM ** -0.5

    for hk in range(N_KV_HEADS):
        c = slice(hk * HEAD_DIM, (hk + 1) * HEAD_DIM)
        kwin = jnp.concatenate([kp_ref[0, :, c], kc_ref[0, :, c], kn_ref[0, :, c]], axis=0)
        vwin = jnp.concatenate([vp_ref[0, :, c], vc_ref[0, :, c], vn_ref[0, :, c]], axis=0).astype(BF16)
        kr = _rope(kwin, cc_k, ss_k).astype(BF16)
        qg = jnp.concatenate(
            [q_ref[0, :, (hk * Q_PER_KV + g) * HEAD_DIM:(hk * Q_PER_KV + g + 1) * HEAD_DIM] for g in range(Q_PER_KV)],
            axis=0,
        )
        qr = _rope(qg, cc_q, ss_q).astype(BF16)
        s = lax.dot_general(qr, kr, (((1,), (1,)), ((), ())), preferred_element_type=F32) * scale
        s = jnp.where(valid, s, NEG_BIG)
        ps, dens = [], []
        for g in range(Q_PER_KV):
            sg = s[g * BLOCK:(g + 1) * BLOCK]
            sink = sink_ref[hk * Q_PER_KV + g]
            m = jnp.maximum(jnp.max(sg, axis=-1, keepdims=True), sink)
            p = jnp.exp(sg - m)
            dens.append(jnp.sum(p, axis=-1, keepdims=True) + jnp.exp(sink - m))
            ps.append(p.astype(BF16))
        pv = _dot(jnp.concatenate(ps, axis=0), vwin)
        for g in range(Q_PER_KV):
            h = hk * Q_PER_KV + g
            o = pv[g * BLOCK:(g + 1) * BLOCK] / dens[g]
            o_ref[0, :, h * HEAD_DIM:(h + 1) * HEAD_DIM] = o.astype(o_ref.dtype)


def windowed_attention(proj3, sink, cc, ss):
    nseq, l, _ = proj3.shape
    nb = l // BLOCK
    kcol, vcol = OFF_K // KV_W, OFF_V // KV_W
    prev = lambda n: jnp.maximum(n - 1, 0)
    nxt = lambda n: jnp.minimum(n + 1, nb - 1)
    kv = lambda rowf, col: pl.BlockSpec((1, BLOCK, KV_W), lambda b, n: (b, rowf(n), col))
    ident = lambda n: n
    return pl.pallas_call(
        _attn_kernel,
        grid=(nseq, nb),
        in_specs=[
            pl.BlockSpec(memory_space=pltpu.SMEM),
            pl.BlockSpec((1, BLOCK, ATTN_W), lambda b, n: (b, n, OFF_Q // ATTN_W)),
            kv(prev, kcol), kv(ident, kcol), kv(nxt, kcol),
            kv(prev, vcol), kv(ident, vcol), kv(nxt, vcol),
            pl.BlockSpec((l, HEAD_DIM), lambda b, n: (0, 0)),
            pl.BlockSpec((l, HEAD_DIM), lambda b, n: (0, 0)),
        ],
        out_specs=pl.BlockSpec((1, BLOCK, ATTN_W), lambda b, n: (b, n, 0)),
        out_shape=jax.ShapeDtypeStruct((nseq, l, ATTN_W), BF16),
        compiler_params=_params("parallel", "arbitrary"),
        name="windowed_attention",
    )(sink, proj3, proj3, proj3, proj3, proj3, proj3, proj3, cc, ss)


HALO = SUBLANES
PAD = D_CONV // 2
CONV_PHASES = 4
CONV_SUB = 64


def _conv_kernel(x_ref, w_ref, b_ref, o_ref, ext_ref, *res_ref):
    rows = x_ref.shape[1]
    nsub = rows // (CONV_PHASES * CONV_SUB)
    ext_ref[0:HALO, :] = jnp.zeros((HALO, LANES), F32)
    ext_ref[HALO + rows:, :] = jnp.zeros((HALO, LANES), F32)
    ext_ref[HALO:HALO + rows, :] = x_ref[0]
    w = [w_ref[j:j + 1, :] for j in range(D_CONV)]
    bias = b_ref[...]

    def body(i, c):
        base = i * (CONV_PHASES * CONV_SUB)
        for r in range(CONV_PHASES):
            acc = bias
            for j in range(D_CONV):
                tap = ext_ref[pl.ds(base + HALO + r + j - PAD, CONV_SUB, stride=CONV_PHASES), :]
                acc = acc + tap * w[j]
            y = acc * _sigmoid(acc)
            if res_ref:
                res_ref[0][pl.ds(base + r, CONV_SUB, stride=CONV_PHASES), :] = y
            else:
                o_ref[0, pl.ds(base + r, CONV_SUB, stride=CONV_PHASES), :] = y
        return c

    lax.fori_loop(0, nsub, body, 0)
    if res_ref:
        o_ref[0] = res_ref[0][...].astype(o_ref.dtype)


def conv_silu(proj3, conv_w, conv_b, *, ch_off, n_ch, out_dtype):
    nseq, l, _ = proj3.shape
    col0 = (OFF_XBC + ch_off) // LANES
    wcol0 = ch_off // LANES
    scratch = [pltpu.VMEM((l + 2 * HALO, LANES), F32)]
    if out_dtype != F32:
        scratch.append(pltpu.VMEM((l, LANES), F32))
    return pl.pallas_call(
        _conv_kernel,
        grid=(nseq, n_ch // LANES),
        in_specs=[
            pl.BlockSpec((1, l, LANES), lambda b, c: (b, 0, col0 + c)),
            pl.BlockSpec((D_CONV, LANES), lambda b, c: (0, wcol0 + c)),
            pl.BlockSpec((1, LANES), lambda b, c: (0, wcol0 + c)),
        ],
        out_specs=pl.BlockSpec((1, l, LANES), lambda b, c: (b, 0, c)),
        out_shape=jax.ShapeDtypeStruct((nseq, l, n_ch), out_dtype),
        scratch_shapes=scratch,
        compiler_params=_params("parallel", "parallel"),
        name="conv_silu",
    )(proj3, conv_w, conv_b)


DT_CHUNKS = 8
LOG2E = 1.4426950408889634
PAIRS = HEADS_PER_GROUP // 2
DH = 2 * SSD_HEADS
COL_AF, COL_AB, COL_GF, COL_GB = (k * HEADS_PER_GROUP for k in range(4))
COL_USED = 4 * HEADS_PER_GROUP


def _split3(a):
    hi = a.astype(BF16)
    r = a - hi.astype(F32)
    mid = r.astype(BF16)
    lo = (r - mid.astype(F32)).astype(BF16)
    return hi, mid, lo


def _tri_sum(tri, a):
    hi, mid, lo = _split3(a)
    return _dot(tri, hi) + _dot(tri, mid) + _dot(tri, lo)


def _pair_rows(v, row_lo, lane_even):
    swap = lambda u: jnp.concatenate([u[CHUNK // 2:], u[:CHUNK // 2]], axis=0)
    from_prev_lane = swap(pltpu.roll(v, 1, 1))
    from_next_lane = swap(pltpu.roll(v, DH - 1, 1))
    w = jnp.where(row_lo, jnp.where(lane_even, v, from_prev_lane), jnp.where(lane_even, from_next_lane, v))
    return w.T


def _dt_kernel(raw_ref, bias_ref, alog_ref, col_ref, rowg_ref, rowdt_ref, tot_ref):
    neg_a2 = -jnp.exp(alog_ref[...]) * LOG2E
    bias = bias_ref[...]
    r = lax.broadcasted_iota(jnp.int32, (CHUNK, CHUNK), 0)
    c = lax.broadcasted_iota(jnp.int32, (CHUNK, CHUNK), 1)
    lower = (r >= c).astype(BF16)
    upper = (r <= c).astype(BF16)
    fwd_lane = c < SSD_HEADS
    row_lo = r < CHUNK // 2
    lane_even = (c & 1) == 0
    src0 = ((c >> 3) & 1) * SSD_HEADS + (c & (HEADS_PER_GROUP - 1))
    for ci in range(raw_ref.shape[0] // CHUNK):
        rows = slice(ci * CHUNK, (ci + 1) * CHUNK)
        raw = raw_ref[rows, :] + bias
        dt = jnp.maximum(raw, 0.0) + jnp.log1p(jnp.exp(-jnp.abs(raw)))
        a2 = dt * neg_a2
        pre = _tri_sum(lower, a2)
        suf = _tri_sum(upper, a2)
        acum2 = jnp.where(fwd_lane, pre, suf)
        g = jnp.log2(dt) - acum2
        tot_ref[ci] = jnp.where(fwd_lane[0:1], pre[CHUNK - 1:CHUNK], suf[0:1])
        rowg_ref[ci] = _pair_rows(g, row_lo, lane_even)
        rowdt_ref[ci] = _pair_rows(dt, row_lo, lane_even)
        for grp in range(SSD_GROUPS):
            idx = src0 + grp * HEADS_PER_GROUP
            packed = jnp.where(c < COL_GF, jnp.take_along_axis(acum2, idx, axis=1),
                               jnp.take_along_axis(g, idx, axis=1))
            col_ref[grp, rows, :] = jnp.where(c < COL_USED, packed, 0.0)


def dt_prepare(proj, dt_bias, a_log):
    t = proj.shape[0]
    rows = DT_CHUNKS * CHUNK
    if t % rows:
        rows = CHUNK
    cps = rows // CHUNK
    nchunks = t // CHUNK
    col = OFF_DT // DH
    return pl.pallas_call(
        _dt_kernel,
        grid=(t // rows,),
        in_specs=[
            pl.BlockSpec((rows, DH), lambda i: (i, col)),
            pl.BlockSpec((1, DH), lambda i: (0, 0)),
            pl.BlockSpec((1, DH), lambda i: (0, 0)),
        ],
        out_specs=[
            pl.BlockSpec((SSD_GROUPS, rows, LANES), lambda i: (0, i, 0)),
            pl.BlockSpec((cps, DH, CHUNK), lambda i: (i, 0, 0)),
            pl.BlockSpec((cps, DH, CHUNK), lambda i: (i, 0, 0)),
            pl.BlockSpec((cps, 1, DH), lambda i: (i, 0, 0)),
        ],
        out_shape=[
            jax.ShapeDtypeStruct((SSD_GROUPS, t, LANES), F32),
            jax.ShapeDtypeStruct((nchunks, DH, CHUNK), F32),
            jax.ShapeDtypeStruct((nchunks, DH, CHUNK), F32),
            jax.ShapeDtypeStruct((nchunks, 1, DH), F32),
        ],
        compiler_params=_params("parallel"),
        name="dt_prepare",
    )(proj, dt_bias, a_log)


SSD_CPS = 4
HALF = CHUNK // 2


def _expand(colv, base, lane_lo):
    tiles = []
    for p in range(PAIRS):
        idx = jnp.where(lane_lo, base + 2 * p, base + 2 * p + 1)
        tiles.append(jnp.take_along_axis(colv, idx, axis=1))
    return jnp.concatenate(tiles, axis=1)


def _ssd_kernel(x_ref, b_ref, c_ref, z_ref, col_ref, rowgf_ref, rowgb_ref, rowdtf_ref,
                tot_ref, skip_ref, ng_ref, o_ref, sf_ref, sb_ref, sb_all_ref):
    ph = pl.program_id(2)
    cs = pl.program_id(3)
    ncs = pl.num_programs(3)
    cps = x_ref.shape[1] // CHUNK
    lane = lax.broadcasted_iota(jnp.int32, (CHUNK, LANES), 1)
    lane_lo = lane < SSD_HEAD_DIM

    def state_update(s_ref, xs, bc, g_x, tot_row):
        xw = (xs * jnp.exp2(tot_row + g_x)).astype(BF16)
        st = lax.dot_general(bc, xw, (((0,), (0,)), ((), ())), preferred_element_type=F32)
        s_ref[...] = s_ref[...] * jnp.exp2(tot_row) + st

    @pl.when(ph == 0)
    def _backward_states():
        @pl.when(cs == 0)
        def _():
            sb_ref[...] = jnp.zeros_like(sb_ref)

        for k in range(cps):
            ci = cps - 1 - k
            rows = slice(ci * CHUNK, (ci + 1) * CHUNK)
            chunk = (ncs - 1 - cs) * cps + ci
            sb_all_ref[chunk] = sb_ref[...].astype(BF16)
            gb = _expand(col_ref[0, 0, rows, :], COL_GB, lane_lo)
            state_update(sb_ref, x_ref[0, rows, :], b_ref[0, rows, :], gb, tot_ref[0, ci, 1])

    @pl.when(ph == 1)
    def _forward_and_output():
        @pl.when(cs == 0)
        def _():
            sf_ref[...] = jnp.zeros_like(sf_ref)

        li = lax.broadcasted_iota(jnp.int32, (CHUNK, LANES), 0)
        s_pos = [half * HALF + (lane & (HALF - 1)) for half in (0, 1)]
        fwd = [li > s for s in s_pos]
        diag = [li == s for s in s_pos]
        for ci in range(cps):
            rows = slice(ci * CHUNK, (ci + 1) * CHUNK)
            chunk = cs * cps + ci
            xs = x_ref[0, rows, :]
            xs16 = xs.astype(BF16)
            bc = b_ref[0, rows, :]
            cc = c_ref[0, rows, :]
            cb = lax.dot_general(cc, bc, (((1,), (1,)), ((), ())), preferred_element_type=F32)
            cb_rot = pltpu.roll(cb, HALF, 1)
            cbs = [jnp.where(lane_lo, cb, cb_rot), jnp.where(lane_lo, cb_rot, cb)]
            colv = col_ref[0, 0, rows, :]
            af = _expand(colv, COL_AF, lane_lo)
            ab = _expand(colv, COL_AB, lane_lo)
            gf = _expand(colv, COL_GF, lane_lo)

            ydiag = []
            for p in range(PAIRS):
                afp = af[:, p * LANES:(p + 1) * LANES]
                abp = ab[:, p * LANES:(p + 1) * LANES]
                tiles = []
                for half in (0, 1):
                    rr = slice(2 * p + half, 2 * p + half + 1)
                    e = jnp.exp2(jnp.where(fwd[half], afp + rowgf_ref[0, ci, rr, :], abp + rowgb_ref[0, ci, rr, :]))
                    e = e + jnp.where(diag[half], rowdtf_ref[0, ci, rr, :], 0.0)
                    tiles.append((cbs[half] * e).astype(BF16))
                xp = xs16[:, p * LANES:(p + 1) * LANES]
                zero = jnp.zeros_like(xp)
                x_lo = jnp.where(lane_lo, xp, zero)
                x_hi = jnp.where(lane_lo, zero, xp)
                rhs = jnp.concatenate([x_lo[:HALF], x_hi[:HALF], x_lo[HALF:], x_hi[HALF:]], axis=0)
                ydiag.append(_dot(jnp.concatenate(tiles, axis=1), rhs))
            y = jnp.concatenate(ydiag, axis=1)

            y = y + _dot(cc, sf_ref[...].astype(BF16)) * jnp.exp2(af)
            y = y + _dot(cc, sb_all_ref[chunk]) * jnp.exp2(ab)
            y = y + xs * skip_ref[...]

            state_update(sf_ref, xs, bc, gf, tot_ref[0, ci, 0])

            z = z_ref[0, rows, :]
            y = y * (z * _sigmoid(z))
            ms = jnp.mean(y * y, axis=-1, keepdims=True)
            o_ref[0, rows, :] = ((y * lax.rsqrt(ms + EPS)) * ng_ref[...]).astype(o_ref.dtype)


def ssd_scan(xs, bc, proj3, colpack, rowg, rowdt, tot_x, skip_x, norm_g):
    nseq, l, _ = xs.shape
    nchunks = l // CHUNK
    cps = SSD_CPS if nchunks % SSD_CPS == 0 else 1
    ncs = nchunks // cps
    rows = cps * CHUNK
    blk = lambda ph, cs: jnp.where(ph == 0, ncs - 1 - cs, cs)
    fwd_only = lambda ph, cs: jnp.where(ph == 0, 0, cs)
    zcol = OFF_Z // GROUP_W
    row_spec = lambda d: pl.BlockSpec(
        (1, cps, HEADS_PER_GROUP, CHUNK), lambda b, g, ph, cs: (b, blk(ph, cs), d * SSD_GROUPS + g, 0))
    return pl.pallas_call(
        _ssd_kernel,
        grid=(nseq, SSD_GROUPS, 2, ncs),
        in_specs=[
            pl.BlockSpec((1, rows, GROUP_W), lambda b, g, ph, cs: (b, blk(ph, cs), g)),
            pl.BlockSpec((1, rows, D_STATE), lambda b, g, ph, cs: (b, blk(ph, cs), g)),
            pl.BlockSpec((1, rows, D_STATE), lambda b, g, ph, cs: (b, fwd_only(ph, cs), SSD_GROUPS + g)),
            pl.BlockSpec((1, rows, GROUP_W), lambda b, g, ph, cs: (b, fwd_only(ph, cs), zcol + g)),
            pl.BlockSpec((1, 1, rows, LANES), lambda b, g, ph, cs: (g, b, blk(ph, cs), 0)),
            row_spec(0), row_spec(1), row_spec(0),
            pl.BlockSpec((1, cps, 2, 1, GROUP_W), lambda b, g, ph, cs: (b, blk(ph, cs), 0, 0, g)),
            pl.BlockSpec((1, GROUP_W), lambda b, g, ph, cs: (0, g)),
            pl.BlockSpec((1, GROUP_W), lambda b, g, ph, cs: (0, g)),
        ],
        out_specs=pl.BlockSpec((1, rows, GROUP_W), lambda b, g, ph, cs: (b, fwd_only(ph, cs), g)),
        out_shape=jax.ShapeDtypeStruct((nseq, l, D_INNER), BF16),
        scratch_shapes=[
            pltpu.VMEM((D_STATE, GROUP_W), F32),
            pltpu.VMEM((D_STATE, GROUP_W), F32),
            pltpu.VMEM((nchunks, D_STATE, GROUP_W), BF16),
        ],
        compiler_params=_params("parallel", "parallel", "arbitrary", "arbitrary"),
        name="ssd_scan",
    )(xs, bc, bc, proj3, colpack, rowg, rowg, rowdt, tot_x, skip_x, norm_g)


def _rope_tables(l):
    half = HEAD_DIM // 2
    inv_freq = ROPE_THETA ** (-jnp.arange(half, dtype=F32) * 2.0 / HEAD_DIM)
    ang = jnp.arange(l, dtype=F32)[:, None] * inv_freq[None, :]
    cos, sin = jnp.cos(ang), jnp.sin(ang)
    return jnp.concatenate([cos, cos], axis=-1), jnp.concatenate([-sin, sin], axis=-1)


def _row_tile(t, want):
    while t % want:
        want //= 2
    return want


def _layer(x, nseq, l, cc, ss, w_in, conv_w, conv_b, a_log, dt_bias, skip_x, ssd_norm, attn_sink,
           w_out_attn, w_out_ssd, w_out, norm_mix, norm_ffn, w_gate_up, w_down):
    t = nseq * l
    nchunks = l // CHUNK
    proj = norm_matmul(x, norm_mix, w_in, tm=_row_tile(t, 1024), tn=IN_TILE, out_dtype=F32)
    proj3 = proj.reshape(nseq, l, N_IN_PAD)

    attn = windowed_attention(proj3, attn_sink, cc, ss).reshape(t, ATTN_W)

    xs = conv_silu(proj3, conv_w, conv_b, ch_off=0, n_ch=D_INNER, out_dtype=F32)
    bcm = conv_silu(proj3, conv_w, conv_b, ch_off=D_INNER, n_ch=CONV_CH - D_INNER, out_dtype=BF16)

    colpack, rowg, rowdt, tot = dt_prepare(proj, dt_bias, a_log)
    tot_x = jnp.repeat(tot.reshape(nseq, nchunks, 2, SSD_HEADS), SSD_HEAD_DIM, axis=-1)
    tot_x = tot_x.reshape(nseq, nchunks, 2, 1, D_INNER)
    colpack = colpack.reshape(SSD_GROUPS, nseq, l, LANES)
    rowg = rowg.reshape(nseq, nchunks, DH, CHUNK)
    rowdt = rowdt.reshape(nseq, nchunks, DH, CHUNK)
    ssd = ssd_scan(xs, bcm, proj3, colpack, rowg, rowdt, tot_x, skip_x, ssd_norm).reshape(t, D_INNER)

    mix = gated_mix(attn, ssd, w_out_attn, w_out_ssd, proj, tm=_row_tile(t, 1024), tn=256)
    x = resid_matmul(mix, w_out, x, tm=_row_tile(t, 1024), tn=1024)
    act = norm_swiglu(x, norm_ffn, w_gate_up, tm=_row_tile(t, 1024), tn=512)
    x = resid_matmul(act, w_down, x, tm=_row_tile(t, 1024), tn=512)
    return x


def _trunk(x, w_in, conv_w, conv_b, a_log, dt_bias, d_skip, ssd_norm, attn_sink,
           w_out_attn, w_out_ssd, w_out, norm_mix, norm_ffn, w_gate_up, w_down):
    nseq, l, _ = x.shape
    x = x.reshape(nseq * l, D_MODEL)
    cc, ss = _rope_tables(l)
    w_in16 = jnp.pad(w_in, ((0, 0), (0, 0), (0, N_IN_PAD - N_IN))).astype(BF16)
    for i in range(DEPTH):
        x = _layer(
            x, nseq, l, cc, ss, w_in16[i], conv_w[i], conv_b[i][None, :],
            a_log[i].reshape(1, -1), dt_bias[i].reshape(1, -1), jnp.repeat(d_skip[i], SSD_HEAD_DIM)[None, :],
            ssd_norm[i][None, :], attn_sink[i],
            w_out_attn[i].astype(BF16), w_out_ssd[i].astype(BF16), w_out[i].astype(BF16),
            norm_mix[i][None, :], norm_ffn[i][None, :], w_gate_up[i].astype(BF16), w_down[i].astype(BF16))
    return x


def kernel(x_prompt, x_sample, w_in, conv_w, conv_b, a_log, dt_bias, d_skip, ssd_norm, attn_sink,
           w_out_attn, w_out_ssd, w_out, norm_mix, norm_ffn, w_gate_up, w_down, final_norm):
    weights = (w_in, conv_w, conv_b, a_log, dt_bias, d_skip, ssd_norm, attn_sink,
               w_out_attn, w_out_ssd, w_out, norm_mix, norm_ffn, w_gate_up, w_down)
    g = final_norm[None, :]
    outs = []
    if x_prompt.shape[1] == x_sample.shape[1]:
        l = x_prompt.shape[1]
        x = _trunk(jnp.concatenate([x_prompt, x_sample], axis=0), *weights)
        row0 = 0
        for xin in (x_prompt, x_sample):
            n = xin.shape[0] * l
            outs.append(rmsnorm(x, g, tm=_row_tile(l, 512), row0=row0, nrows=n).reshape(xin.shape))
            row0 += n
    else:
        for xin in (x_prompt, x_sample):
            x = _trunk(xin, *weights)
            outs.append(rmsnorm(x, g, tm=_row_tile(x.shape[0], 512), row0=0, nrows=x.shape[0]).reshape(xin.shape))
    return tuple(outs)
```

```python
import jax
import jax.numpy as jnp
from jax import lax
from jax.experimental import pallas as pl
from jax.experimental.pallas import tpu as pltpu

D_MODEL = 2048
DEPTH = 4
N_HEADS = 16
N_KV_HEADS = 4
HEAD_DIM = 128
Q_PER_KV = N_HEADS // N_KV_HEADS
ATTN_W = N_HEADS * HEAD_DIM
KV_W = N_KV_HEADS * HEAD_DIM
BLOCK = 128
ROPE_THETA = 10000.0
D_INNER = 2 * D_MODEL
SSD_HEAD_DIM = 64
SSD_HEADS = D_INNER // SSD_HEAD_DIM
SSD_GROUPS = 8
HEADS_PER_GROUP = SSD_HEADS // SSD_GROUPS
GROUP_W = D_INNER // SSD_GROUPS
D_STATE = 128
D_CONV = 5
CHUNK = 128
CONV_CH = D_INNER + 2 * SSD_GROUPS * D_STATE
D_FF = ((8 * D_MODEL // 3 + 255) // 256) * 256
EPS = 1e-6

OFF_Q = 0
OFF_K = OFF_Q + ATTN_W
OFF_V = OFF_K + KV_W
OFF_GATT = OFF_V + KV_W
OFF_GSSD = OFF_GATT + D_MODEL
OFF_Z = OFF_GSSD + D_MODEL
OFF_XBC = OFF_Z + D_INNER
OFF_DT = OFF_XBC + CONV_CH
N_IN = OFF_DT + 2 * SSD_HEADS
IN_TILE = 768
N_IN_PAD = ((N_IN + IN_TILE - 1) // IN_TILE) * IN_TILE

LANES = 128
SUBLANES = 8
VMEM_LIMIT = 56 * 1024 * 1024
NEG_BIG = -1e30
LOG2E = 1.4426950408889634

F32 = jnp.float32
BF16 = jnp.bfloat16


def _params(*sem):
    return pltpu.CompilerParams(dimension_semantics=sem, vmem_limit_bytes=VMEM_LIMIT)


def _sigmoid(x):
    return 1.0 / (1.0 + jnp.exp(-x))


def _dot(a, b):
    return jnp.dot(a, b, preferred_element_type=F32)


NORM_ROWS = 128


def _norm_into(x_ref, g_ref, h_ref):
    g = g_ref[...]

    def body(i, c):
        r = pl.ds(pl.multiple_of(i * NORM_ROWS, NORM_ROWS), NORM_ROWS)
        x = x_ref[r, :]
        ms = jnp.mean(x * x, axis=-1, keepdims=True)
        h_ref[r, :] = ((x * lax.rsqrt(ms + EPS)) * g).astype(BF16)
        return c

    lax.fori_loop(0, x_ref.shape[0] // NORM_ROWS, body, 0)


def _norm_matmul_kernel(x_ref, g_ref, w_ref, o_ref, h_ref):
    @pl.when(pl.program_id(1) == 0)
    def _():
        _norm_into(x_ref, g_ref, h_ref)

    o_ref[...] = _dot(h_ref[...], w_ref[...]).astype(o_ref.dtype)


def norm_matmul(x, g, w, layer, *, tm, tn, out_dtype):
    t, k = x.shape
    n = w.shape[2]
    return pl.pallas_call(
        _norm_matmul_kernel,
        grid=(t // tm, n // tn),
        in_specs=[
            pl.BlockSpec((tm, k), lambda i, j: (i, 0)),
            pl.BlockSpec((1, k), lambda i, j: (0, 0)),
            pl.BlockSpec((None, k, tn), lambda i, j: (layer, 0, j)),
        ],
        out_specs=pl.BlockSpec((tm, tn), lambda i, j: (i, j)),
        out_shape=jax.ShapeDtypeStruct((t, n), out_dtype),
        scratch_shapes=[pltpu.VMEM((tm, k), BF16)],
        compiler_params=_params("parallel", "arbitrary"),
        name="norm_in_proj",
    )(x, g, w)


def _norm_swiglu_kernel(x_ref, g_ref, wg_ref, wu_ref, o_ref, h_ref):
    @pl.when(pl.program_id(1) == 0)
    def _():
        _norm_into(x_ref, g_ref, h_ref)

    h = h_ref[...]
    gt = _dot(h, wg_ref[...])
    up = _dot(h, wu_ref[...])
    o_ref[...] = ((gt * _sigmoid(gt)) * up).astype(o_ref.dtype)


def norm_swiglu(x, g, w_gate_up, layer, *, tm, tn):
    t, k = x.shape
    f = w_gate_up.shape[2] // 2
    nf = f // tn
    return pl.pallas_call(
        _norm_swiglu_kernel,
        grid=(t // tm, nf),
        in_specs=[
            pl.BlockSpec((tm, k), lambda i, j: (i, 0)),
            pl.BlockSpec((1, k), lambda i, j: (0, 0)),
            pl.BlockSpec((None, k, tn), lambda i, j: (layer, 0, j)),
            pl.BlockSpec((None, k, tn), lambda i, j: (layer, 0, j + nf)),
        ],
        out_specs=pl.BlockSpec((tm, tn), lambda i, j: (i, j)),
        out_shape=jax.ShapeDtypeStruct((t, f), BF16),
        scratch_shapes=[pltpu.VMEM((tm, k), BF16)],
        compiler_params=_params("parallel", "arbitrary"),
        name="norm_swiglu",
    )(x, g, w_gate_up, w_gate_up)


def _resid_matmul_kernel(a_ref, w_ref, x_ref, o_ref):
    o_ref[...] = x_ref[...] + _dot(a_ref[...], w_ref[...])


def resid_matmul(a, w, layer, x, *, tm, tn):
    t, k = a.shape
    n = w.shape[2]
    return pl.pallas_call(
        _resid_matmul_kernel,
        grid=(t // tm, n // tn),
        in_specs=[
            pl.BlockSpec((tm, k), lambda i, j: (i, 0)),
            pl.BlockSpec((None, k, tn), lambda i, j: (layer, 0, j)),
            pl.BlockSpec((tm, tn), lambda i, j: (i, j)),
        ],
        out_specs=pl.BlockSpec((tm, tn), lambda i, j: (i, j)),
        out_shape=jax.ShapeDtypeStruct((t, n), F32),
        compiler_params=_params("parallel", "arbitrary"),
        name="resid_matmul",
    )(a, w, x)


def _mix_kernel(attn_ref, ssd_ref, wa_ref, ws_ref, ga_ref, gs_ref, o_ref):
    a = _dot(attn_ref[...], wa_ref[...])
    s = _dot(ssd_ref[...], ws_ref[...])
    o_ref[...] = (_sigmoid(ga_ref[...]) * a + _sigmoid(gs_ref[...]) * s).astype(o_ref.dtype)


def gated_mix(attn, ssd, wa, ws, layer, proj, *, tm, tn):
    t = attn.shape[0]
    n = wa.shape[2]
    return pl.pallas_call(
        _mix_kernel,
        grid=(t // tm, n // tn),
        in_specs=[
            pl.BlockSpec((tm, attn.shape[1]), lambda i, j: (i, 0)),
            pl.BlockSpec((tm, ssd.shape[1]), lambda i, j: (i, 0)),
            pl.BlockSpec((None, wa.shape[1], tn), lambda i, j: (layer, 0, j)),
            pl.BlockSpec((None, ws.shape[1], tn), lambda i, j: (layer, 0, j)),
            pl.BlockSpec((tm, tn), lambda i, j: (i, j + OFF_GATT // tn)),
            pl.BlockSpec((tm, tn), lambda i, j: (i, j + OFF_GSSD // tn)),
        ],
        out_specs=pl.BlockSpec((tm, tn), lambda i, j: (i, j)),
        out_shape=jax.ShapeDtypeStruct((t, n), BF16),
        compiler_params=_params("parallel", "arbitrary"),
        name="gated_mix",
    )(attn, ssd, wa, ws, proj, proj)


def _rmsnorm_kernel(x_ref, g_ref, o_ref):
    x = x_ref[...]
    ms = jnp.mean(x * x, axis=-1, keepdims=True)
    o_ref[...] = (x * lax.rsqrt(ms + EPS)) * g_ref[...]


def rmsnorm(x, g, *, tm, row0, nrows):
    k = x.shape[1]
    off = row0 // tm
    return pl.pallas_call(
        _rmsnorm_kernel,
        grid=(nrows // tm,),
        in_specs=[pl.BlockSpec((tm, k), lambda i: (i + off, 0)), pl.BlockSpec((1, k), lambda i: (0, 0))],
        out_specs=pl.BlockSpec((tm, k), lambda i: (i, 0)),
        out_shape=jax.ShapeDtypeStruct((nrows, k), F32),
        compiler_params=_params("parallel"),
        name="final_rmsnorm",
    )(x, g)


def _rope(x, cc, ss):
    return x * cc + pltpu.roll(x, HEAD_DIM // 2, 1) * ss


def _attn_kernel(sink_ref, q_ref, kp_ref, kc_ref, kn_ref, vp_ref, vc_ref, vn_ref, cc_ref, ss_ref, o_ref):
    n = pl.program_id(1)
    nb = pl.num_programs(1)
    pos_c = pl.multiple_of(n * BLOCK, BLOCK)
    pos_p = pl.multiple_of(jnp.maximum(n - 1, 0) * BLOCK, BLOCK)
    pos_n = pl.multiple_of(jnp.minimum(n + 1, nb - 1) * BLOCK, BLOCK)
    cc_c, ss_c = cc_ref[pl.ds(pos_c, BLOCK), :], ss_ref[pl.ds(pos_c, BLOCK), :]
    cc_k = jnp.concatenate([cc_ref[pl.ds(pos_p, BLOCK), :], cc_c, cc_ref[pl.ds(pos_n, BLOCK), :]], axis=0)
    ss_k = jnp.concatenate([ss_ref[pl.ds(pos_p, BLOCK), :], ss_c, ss_ref[pl.ds(pos_n, BLOCK), :]], axis=0)
    cc_q = jnp.concatenate([cc_c] * Q_PER_KV, axis=0)
    ss_q = jnp.concatenate([ss_c] * Q_PER_KV, axis=0)

    rows = Q_PER_KV * BLOCK
    t = lax.broadcasted_iota(jnp.int32, (rows, BLOCK), 0) % BLOCK
    sk = lax.broadcasted_iota(jnp.int32, (rows, BLOCK), 1)
    valid_prev = sk >= t + jnp.where(n > 0, 0, BLOCK)
    valid_next = sk <= t - jnp.where(n < nb - 1, 0, BLOCK)
    scale2 = HEAD_DIM ** -0.5 * LOG2E

    for hk in range(N_KV_HEADS):
        c = slice(hk * HEAD_DIM, (hk + 1) * HEAD_DIM)
        kwin = jnp.concatenate([kp_ref[0, :, c], kc_ref[0, :, c], kn_ref[0, :, c]], axis=0)
        vwin = jnp.concatenate([vp_ref[0, :, c], vc_ref[0, :, c], vn_ref[0, :, c]], axis=0).astype(BF16)
        kr = _rope(kwin, cc_k, ss_k).astype(BF16)
        qg = jnp.concatenate(
            [q_ref[0, :, (hk * Q_PER_KV + g) * HEAD_DIM:(hk * Q_PER_KV + g + 1) * HEAD_DIM] for g in range(Q_PER_KV)],
            axis=0,
        )
        qr = _rope(qg, cc_q, ss_q).astype(BF16)
        s = lax.dot_general(qr, kr, (((1,), (1,)), ((), ())), preferred_element_type=F32) * scale2
        s = jnp.concatenate([jnp.where(valid_prev, s[:, :BLOCK], NEG_BIG), s[:, BLOCK:2 * BLOCK],
                             jnp.where(valid_next, s[:, 2 * BLOCK:], NEG_BIG)], axis=1)
        ps, dens = [], []
        for g in range(Q_PER_KV):
            sg = s[g * BLOCK:(g + 1) * BLOCK]
            sink2 = sink_ref[hk * Q_PER_KV + g] * LOG2E
            m = jnp.maximum(jnp.max(sg, axis=-1, keepdims=True), sink2)
            p = jnp.exp2(sg - m)
            dens.append(jnp.sum(p, axis=-1, keepdims=True) + jnp.exp2(sink2 - m))
            ps.append(p.astype(BF16))
        pv = _dot(jnp.concatenate(ps, axis=0), vwin)
        for g in range(Q_PER_KV):
            h = hk * Q_PER_KV + g
            o = pv[g * BLOCK:(g + 1) * BLOCK] / dens[g]
            o_ref[0, :, h * HEAD_DIM:(h + 1) * HEAD_DIM] = o.astype(o_ref.dtype)


def windowed_attention(proj3, sink, cc, ss):
    nseq, l, _ = proj3.shape
    nb = l // BLOCK
    kcol, vcol = OFF_K // KV_W, OFF_V // KV_W
    prev = lambda n: jnp.maximum(n - 1, 0)
    nxt = lambda n: jnp.minimum(n + 1, nb - 1)
    kv = lambda rowf, col: pl.BlockSpec((1, BLOCK, KV_W), lambda b, n: (b, rowf(n), col))
    ident = lambda n: n
    return pl.pallas_call(
        _attn_kernel,
        grid=(nseq, nb),
        in_specs=[
            pl.BlockSpec(memory_space=pltpu.SMEM),
            pl.BlockSpec((1, BLOCK, ATTN_W), lambda b, n: (b, n, OFF_Q // ATTN_W)),
            kv(prev, kcol), kv(ident, kcol), kv(nxt, kcol),
            kv(prev, vcol), kv(ident, vcol), kv(nxt, vcol),
            pl.BlockSpec((l, HEAD_DIM), lambda b, n: (0, 0)),
            pl.BlockSpec((l, HEAD_DIM), lambda b, n: (0, 0)),
        ],
        out_specs=pl.BlockSpec((1, BLOCK, ATTN_W), lambda b, n: (b, n, 0)),
        out_shape=jax.ShapeDtypeStruct((nseq, l, ATTN_W), BF16),
        compiler_params=_params("parallel", "arbitrary"),
        name="windowed_attention",
    )(sink, proj3, proj3, proj3, proj3, proj3, proj3, proj3, cc, ss)


HALO = SUBLANES
PAD = D_CONV // 2
CONV_PHASES = 4
CONV_SUB = 64


def _conv_kernel(x_ref, w_ref, b_ref, o_ref, ext_ref, *res_ref):
    rows = x_ref.shape[1]
    nsub = rows // (CONV_PHASES * CONV_SUB)
    ext_ref[0:HALO, :] = jnp.zeros((HALO, LANES), F32)
    ext_ref[HALO + rows:, :] = jnp.zeros((HALO, LANES), F32)
    ext_ref[HALO:HALO + rows, :] = x_ref[0]
    w = [w_ref[j:j + 1, :] for j in range(D_CONV)]
    bias = b_ref[...]

    def body(i, c):
        base = i * (CONV_PHASES * CONV_SUB)
        for r in range(CONV_PHASES):
            acc = bias
            for j in range(D_CONV):
                tap = ext_ref[pl.ds(base + HALO + r + j - PAD, CONV_SUB, stride=CONV_PHASES), :]
                acc = acc + tap * w[j]
            y = acc * _sigmoid(acc)
            if res_ref:
                res_ref[0][pl.ds(base + r, CONV_SUB, stride=CONV_PHASES), :] = y
            else:
                o_ref[0, pl.ds(base + r, CONV_SUB, stride=CONV_PHASES), :] = y
        return c

    lax.fori_loop(0, nsub, body, 0)
    if res_ref:
        o_ref[0] = res_ref[0][...].astype(o_ref.dtype)


def conv_silu(proj3, conv_w, conv_b, *, ch_off, n_ch, out_dtype):
    nseq, l, _ = proj3.shape
    col0 = (OFF_XBC + ch_off) // LANES
    wcol0 = ch_off // LANES
    scratch = [pltpu.VMEM((l + 2 * HALO, LANES), F32)]
    if out_dtype != F32:
        scratch.append(pltpu.VMEM((l, LANES), F32))
    return pl.pallas_call(
        _conv_kernel,
        grid=(nseq, n_ch // LANES),
        in_specs=[
            pl.BlockSpec((1, l, LANES), lambda b, c: (b, 0, col0 + c)),
            pl.BlockSpec((D_CONV, LANES), lambda b, c: (0, wcol0 + c)),
            pl.BlockSpec((1, LANES), lambda b, c: (0, wcol0 + c)),
        ],
        out_specs=pl.BlockSpec((1, l, LANES), lambda b, c: (b, 0, c)),
        out_shape=jax.ShapeDtypeStruct((nseq, l, n_ch), out_dtype),
        scratch_shapes=scratch,
        compiler_params=_params("parallel", "parallel"),
        name="conv_silu",
    )(proj3, conv_w, conv_b)


DT_CHUNKS = 8
PAIRS = HEADS_PER_GROUP // 2
DH = 2 * SSD_HEADS
COL_AF, COL_AB, COL_GF, COL_GB = (k * HEADS_PER_GROUP for k in range(4))
COL_USED = 4 * HEADS_PER_GROUP
GROUP_LANES = SSD_HEADS // 2


def _split3(a):
    hi = a.astype(BF16)
    r = a - hi.astype(F32)
    mid = r.astype(BF16)
    lo = (r - mid.astype(F32)).astype(BF16)
    return hi, mid, lo


def _tri_sum(tri, a):
    hi, mid, lo = _split3(a)
    return _dot(tri, hi) + _dot(tri, mid) + _dot(tri, lo)


def _pair_rows(v, row_lo, lane_even):
    swap = lambda u: jnp.concatenate([u[CHUNK // 2:], u[:CHUNK // 2]], axis=0)
    from_prev_lane = swap(pltpu.roll(v, 1, 1))
    from_next_lane = swap(pltpu.roll(v, DH - 1, 1))
    w = jnp.where(row_lo, jnp.where(lane_even, v, from_prev_lane), jnp.where(lane_even, from_next_lane, v))
    return w.T


def _dt_kernel(raw_ref, bias_ref, alog_ref, col_ref, rowg_ref, rowdt_ref, tot_ref):
    neg_a2 = -jnp.exp(alog_ref[...]) * LOG2E
    bias = bias_ref[...]
    r = lax.broadcasted_iota(jnp.int32, (CHUNK, CHUNK), 0)
    c = lax.broadcasted_iota(jnp.int32, (CHUNK, CHUNK), 1)
    lower = (r >= c).astype(BF16)
    upper = (r <= c).astype(BF16)
    fwd_lane = c < SSD_HEADS
    row_lo = r < CHUNK // 2
    lane_even = (c & 1) == 0
    src0 = ((c >> 3) & 1) * SSD_HEADS + (c & (HEADS_PER_GROUP - 1))
    low_groups = (c & GROUP_LANES) == 0
    for ci in range(raw_ref.shape[0] // CHUNK):
        rows = slice(ci * CHUNK, (ci + 1) * CHUNK)
        raw = raw_ref[rows, :] + bias
        dt = jnp.maximum(raw, 0.0) + jnp.log1p(jnp.exp(-jnp.abs(raw)))
        a2 = dt * neg_a2
        pre = _tri_sum(lower, a2)
        suf = _tri_sum(upper, a2)
        acum2 = jnp.where(fwd_lane, pre, suf)
        g = jnp.log2(dt) - acum2
        tot_ref[ci] = jnp.where(fwd_lane[0:1], pre[CHUNK - 1:CHUNK], suf[0:1])
        rowg_ref[ci] = _pair_rows(g, row_lo, lane_even)
        rowdt_ref[ci] = _pair_rows(dt, row_lo, lane_even)
        sources = (jnp.where(low_groups, acum2, pltpu.roll(g, GROUP_LANES, 1)),
                   jnp.where(low_groups, pltpu.roll(g, DH - GROUP_LANES, 1), acum2))
        for grp in range(SSD_GROUPS):
            hi = grp >= SSD_GROUPS // 2
            idx = src0 + grp * HEADS_PER_GROUP + jnp.where(c < COL_GF, 0, -GROUP_LANES if hi else GROUP_LANES)
            packed = jnp.take_along_axis(sources[hi], idx, axis=1)
            col_ref[grp, rows, :] = jnp.where(c < COL_USED, packed, 0.0)


def dt_prepare(proj, dt_bias, a_log):
    t = proj.shape[0]
    rows = DT_CHUNKS * CHUNK
    if t % rows:
        rows = CHUNK
    cps = rows // CHUNK
    nchunks = t // CHUNK
    col = OFF_DT // DH
    return pl.pallas_call(
        _dt_kernel,
        grid=(t // rows,),
        in_specs=[
            pl.BlockSpec((rows, DH), lambda i: (i, col)),
            pl.BlockSpec((1, DH), lambda i: (0, 0)),
            pl.BlockSpec((1, DH), lambda i: (0, 0)),
        ],
        out_specs=[
            pl.BlockSpec((SSD_GROUPS, rows, LANES), lambda i: (0, i, 0)),
            pl.BlockSpec((cps, DH, CHUNK), lambda i: (i, 0, 0)),
            pl.BlockSpec((cps, DH, CHUNK), lambda i: (i, 0, 0)),
            pl.BlockSpec((cps, 1, DH), lambda i: (i, 0, 0)),
        ],
        out_shape=[
            jax.ShapeDtypeStruct((SSD_GROUPS, t, LANES), F32),
            jax.ShapeDtypeStruct((nchunks, DH, CHUNK), F32),
            jax.ShapeDtypeStruct((nchunks, DH, CHUNK), F32),
            jax.ShapeDtypeStruct((nchunks, 1, DH), F32),
        ],
        compiler_params=_params("parallel"),
        name="dt_prepare",
    )(proj, dt_bias, a_log)


SSD_CPS = 8
HALF = CHUNK // 2


def _expand(colv, base, lane_lo):
    tiles = []
    for p in range(PAIRS):
        idx = jnp.where(lane_lo, base + 2 * p, base + 2 * p + 1)
        tiles.append(jnp.take_along_axis(colv, idx, axis=1))
    return jnp.concatenate(tiles, axis=1)


def _ssd_kernel(x_ref, b_ref, c_ref, z_ref, col_ref, rowgf_ref, rowgb_ref, rowdtf_ref,
                tot_ref, skip_ref, ng_ref, o_ref, sf_ref, sb_ref, sb_all_ref):
    ph = pl.program_id(2)
    cs = pl.program_id(3)
    ncs = pl.num_programs(3)
    cps = x_ref.shape[1] // CHUNK
    lane = lax.broadcasted_iota(jnp.int32, (CHUNK, LANES), 1)
    lane_lo = lane < SSD_HEAD_DIM

    def state_update(s_ref, xs, bc, g_x, tot_row):
        xw = (xs * jnp.exp2(tot_row + g_x)).astype(BF16)
        st = lax.dot_general(bc, xw, (((0,), (0,)), ((), ())), preferred_element_type=F32)
        s_ref[...] = s_ref[...] * jnp.exp2(tot_row) + st

    @pl.when(ph == 0)
    def _backward_states():
        @pl.when(cs == 0)
        def _():
            sb_ref[...] = jnp.zeros_like(sb_ref)

        for k in range(cps):
            ci = cps - 1 - k
            rows = slice(ci * CHUNK, (ci + 1) * CHUNK)
            chunk = (ncs - 1 - cs) * cps + ci
            sb_all_ref[chunk] = sb_ref[...].astype(BF16)
            gb = _expand(col_ref[0, 0, rows, :], COL_GB, lane_lo)
            state_update(sb_ref, x_ref[0, rows, :], b_ref[0, rows, :], gb, tot_ref[0, ci, 1])

    @pl.when(ph == 1)
    def _forward_and_output():
        @pl.when(cs == 0)
        def _():
            sf_ref[...] = jnp.zeros_like(sf_ref)

        li = lax.broadcasted_iota(jnp.int32, (CHUNK, LANES), 0)
        s_pos = [half * HALF + (lane & (HALF - 1)) for half in (0, 1)]
        fwd = [li > s for s in s_pos]
        diag = [li == s for s in s_pos]
        for ci in range(cps):
            rows = slice(ci * CHUNK, (ci + 1) * CHUNK)
            chunk = cs * cps + ci
            xs = x_ref[0, rows, :]
            xs16 = xs.astype(BF16)
            bc = b_ref[0, rows, :]
            cc = c_ref[0, rows, :]
            cb = lax.dot_general(cc, bc, (((1,), (1,)), ((), ())), preferred_element_type=F32)
            cb_rot = pltpu.roll(cb, HALF, 1)
            cbs = [jnp.where(lane_lo, cb, cb_rot), jnp.where(lane_lo, cb_rot, cb)]
            colv = col_ref[0, 0, rows, :]
            af = _expand(colv, COL_AF, lane_lo)
            ab = _expand(colv, COL_AB, lane_lo)
            gf = _expand(colv, COL_GF, lane_lo)

            ydiag = []
            for p in range(PAIRS):
                afp = af[:, p * LANES:(p + 1) * LANES]
                abp = ab[:, p * LANES:(p + 1) * LANES]
                tiles = []
                for half in (0, 1):
                    rr = slice(2 * p + half, 2 * p + half + 1)
                    e = jnp.exp2(jnp.where(fwd[half], afp + rowgf_ref[0, ci, rr, :], abp + rowgb_ref[0, ci, rr, :]))
                    e = e + jnp.where(diag[half], rowdtf_ref[0, ci, rr, :], 0.0)
                    tiles.append((cbs[half] * e).astype(BF16))
                xp = xs16[:, p * LANES:(p + 1) * LANES]
                zero = jnp.zeros_like(xp)
                x_lo = jnp.where(lane_lo, xp, zero)
                x_hi = jnp.where(lane_lo, zero, xp)
                rhs = jnp.concatenate([x_lo[:HALF], x_hi[:HALF], x_lo[HALF:], x_hi[HALF:]], axis=0)
                ydiag.append(_dot(jnp.concatenate(tiles, axis=1), rhs))
            y = jnp.concatenate(ydiag, axis=1)

            y = y + _dot(cc, sf_ref[...].astype(BF16)) * jnp.exp2(af)
            y = y + _dot(cc, sb_all_ref[chunk]) * jnp.exp2(ab)
            y = y + xs * skip_ref[...]

            state_update(sf_ref, xs, bc, gf, tot_ref[0, ci, 0])

            z = z_ref[0, rows, :]
            y = y * (z * _sigmoid(z))
            ms = jnp.mean(y * y, axis=-1, keepdims=True)
            o_ref[0, rows, :] = ((y * lax.rsqrt(ms + EPS)) * ng_ref[...]).astype(o_ref.dtype)


def ssd_scan(xs, bc, proj3, colpack, rowg, rowdt, tot_x, skip_x, norm_g):
    nseq, l, _ = xs.shape
    nchunks = l // CHUNK
    cps = SSD_CPS if nchunks % SSD_CPS == 0 else 1
    ncs = nchunks // cps
    rows = cps * CHUNK
    blk = lambda ph, cs: jnp.where(ph == 0, ncs - 1 - cs, cs)
    fwd_only = lambda ph, cs: jnp.where(ph == 0, 0, cs)
    zcol = OFF_Z // GROUP_W
    row_spec = lambda d: pl.BlockSpec(
        (1, cps, HEADS_PER_GROUP, CHUNK), lambda b, g, ph, cs: (b, blk(ph, cs), d * SSD_GROUPS + g, 0))
    return pl.pallas_call(
        _ssd_kernel,
        grid=(nseq, SSD_GROUPS, 2, ncs),
        in_specs=[
            pl.BlockSpec((1, rows, GROUP_W), lambda b, g, ph, cs: (b, blk(ph, cs), g)),
            pl.BlockSpec((1, rows, D_STATE), lambda b, g, ph, cs: (b, blk(ph, cs), g)),
            pl.BlockSpec((1, rows, D_STATE), lambda b, g, ph, cs: (b, fwd_only(ph, cs), SSD_GROUPS + g)),
            pl.BlockSpec((1, rows, GROUP_W), lambda b, g, ph, cs: (b, fwd_only(ph, cs), zcol + g)),
            pl.BlockSpec((1, 1, rows, LANES), lambda b, g, ph, cs: (g, b, blk(ph, cs), 0)),
            row_spec(0), row_spec(1), row_spec(0),
            pl.BlockSpec((1, cps, 2, 1, GROUP_W), lambda b, g, ph, cs: (b, blk(ph, cs), 0, 0, g)),
            pl.BlockSpec((1, GROUP_W), lambda b, g, ph, cs: (0, g)),
            pl.BlockSpec((1, GROUP_W), lambda b, g, ph, cs: (0, g)),
        ],
        out_specs=pl.BlockSpec((1, rows, GROUP_W), lambda b, g, ph, cs: (b, fwd_only(ph, cs), g)),
        out_shape=jax.ShapeDtypeStruct((nseq, l, D_INNER), BF16),
        scratch_shapes=[
            pltpu.VMEM((D_STATE, GROUP_W), F32),
            pltpu.VMEM((D_STATE, GROUP_W), F32),
            pltpu.VMEM((nchunks, D_STATE, GROUP_W), BF16),
        ],
        compiler_params=_params("parallel", "parallel", "arbitrary", "arbitrary"),
        name="ssd_scan",
    )(xs, bc, bc, proj3, colpack, rowg, rowg, rowdt, tot_x, skip_x, norm_g)


def _rope_tables(l):
    half = HEAD_DIM // 2
    inv_freq = ROPE_THETA ** (-jnp.arange(half, dtype=F32) * 2.0 / HEAD_DIM)
    ang = jnp.arange(l, dtype=F32)[:, None] * inv_freq[None, :]
    cos, sin = jnp.cos(ang), jnp.sin(ang)
    return jnp.concatenate([cos, cos], axis=-1), jnp.concatenate([-sin, sin], axis=-1)


def _row_tile(t, want):
    while t % want:
        want //= 2
    return want


def _layer(layer, x, nseq, l, cc, ss, w_in, conv_w, conv_b, a_log, dt_bias, skip_x, ssd_norm, attn_sink,
           w_out_attn, w_out_ssd, w_out, norm_mix, norm_ffn, w_gate_up, w_down):
    t = nseq * l
    nchunks = l // CHUNK
    proj = norm_matmul(x, norm_mix, w_in, layer, tm=_row_tile(t, 1024), tn=IN_TILE, out_dtype=F32)
    proj3 = proj.reshape(nseq, l, N_IN_PAD)

    attn = windowed_attention(proj3, attn_sink, cc, ss).reshape(t, ATTN_W)

    xs = conv_silu(proj3, conv_w, conv_b, ch_off=0, n_ch=D_INNER, out_dtype=F32)
    bcm = conv_silu(proj3, conv_w, conv_b, ch_off=D_INNER, n_ch=CONV_CH - D_INNER, out_dtype=BF16)

    colpack, rowg, rowdt, tot = dt_prepare(proj, dt_bias, a_log)
    tot_x = jnp.repeat(tot.reshape(nseq, nchunks, 2, SSD_HEADS), SSD_HEAD_DIM, axis=-1)
    tot_x = tot_x.reshape(nseq, nchunks, 2, 1, D_INNER)
    colpack = colpack.reshape(SSD_GROUPS, nseq, l, LANES)
    rowg = rowg.reshape(nseq, nchunks, DH, CHUNK)
    rowdt = rowdt.reshape(nseq, nchunks, DH, CHUNK)
    ssd = ssd_scan(xs, bcm, proj3, colpack, rowg, rowdt, tot_x, skip_x, ssd_norm).reshape(t, D_INNER)

    mix = gated_mix(attn, ssd, w_out_attn, w_out_ssd, layer, proj, tm=_row_tile(t, 1024), tn=256)
    x = resid_matmul(mix, w_out, layer, x, tm=_row_tile(t, 1024), tn=1024)
    act = norm_swiglu(x, norm_ffn, w_gate_up, layer, tm=_row_tile(t, 1024), tn=512)
    x = resid_matmul(act, w_down, layer, x, tm=_row_tile(t, 1024), tn=512)
    return x


def _trunk(x, w_in, conv_w, conv_b, a_log, dt_bias, d_skip, ssd_norm, attn_sink,
           w_out_attn, w_out_ssd, w_out, norm_mix, norm_ffn, w_gate_up, w_down):
    nseq, l, _ = x.shape
    x = x.reshape(nseq * l, D_MODEL)
    cc, ss = _rope_tables(l)
    w_in16 = jnp.pad(w_in, ((0, 0), (0, 0), (0, N_IN_PAD - N_IN))).astype(BF16)
    mats = [w.astype(BF16) for w in (w_out_attn, w_out_ssd, w_out)]
    ffn = [w.astype(BF16) for w in (w_gate_up, w_down)]
    for i in range(DEPTH):
        x = _layer(
            i, x, nseq, l, cc, ss, w_in16, conv_w[i], conv_b[i][None, :],
            a_log[i].reshape(1, -1), dt_bias[i].reshape(1, -1), jnp.repeat(d_skip[i], SSD_HEAD_DIM)[None, :],
            ssd_norm[i][None, :], attn_sink[i], *mats, norm_mix[i][None, :], norm_ffn[i][None, :], *ffn)
    return x


def kernel(x_prompt, x_sample, w_in, conv_w, conv_b, a_log, dt_bias, d_skip, ssd_norm, attn_sink,
           w_out_attn, w_out_ssd, w_out, norm_mix, norm_ffn, w_gate_up, w_down, final_norm):
    weights = (w_in, conv_w, conv_b, a_log, dt_bias, d_skip, ssd_norm, attn_sink,
               w_out_attn, w_out_ssd, w_out, norm_mix, norm_ffn, w_gate_up, w_down)
    g = final_norm[None, :]
    outs = []
    if x_prompt.shape[1] == x_sample.shape[1]:
        l = x_prompt.shape[1]
        x = _trunk(jnp.concatenate([x_prompt, x_sample], axis=0), *weights)
        row0 = 0
        for xin in (x_prompt, x_sample):
            n = xin.shape[0] * l
            outs.append(rmsnorm(x, g, tm=_row_tile(l, 512), row0=row0, nrows=n).reshape(xin.shape))
            row0 += n
    else:
        for xin in (x_prompt, x_sample):
            x = _trunk(xin, *weights)
            outs.append(rmsnorm(x, g, tm=_row_tile(x.shape[0], 512), row0=0, nrows=x.shape[0]).reshape(xin.shape))
    return tuple(outs)
```

```python
import jax
import jax.numpy as jnp
from jax import lax
from jax.experimental import pallas as pl
from jax.experimental.pallas import tpu as pltpu

D_MODEL = 2048
DEPTH = 4
N_HEADS = 16
N_KV_HEADS = 4
HEAD_DIM = 128
Q_PER_KV = N_HEADS // N_KV_HEADS
ATTN_W = N_HEADS * HEAD_DIM
KV_W = N_KV_HEADS * HEAD_DIM
BLOCK = 128
ROPE_THETA = 10000.0
D_INNER = 2 * D_MODEL
SSD_HEAD_DIM = 64
SSD_HEADS = D_INNER // SSD_HEAD_DIM
SSD_GROUPS = 8
HEADS_PER_GROUP = SSD_HEADS // SSD_GROUPS
GROUP_W = D_INNER // SSD_GROUPS
D_STATE = 128
D_CONV = 5
CHUNK = 128
CONV_CH = D_INNER + 2 * SSD_GROUPS * D_STATE
D_FF = ((8 * D_MODEL // 3 + 255) // 256) * 256
EPS = 1e-6

OFF_Q = 0
OFF_K = OFF_Q + ATTN_W
OFF_V = OFF_K + KV_W
OFF_GATT = OFF_V + KV_W
OFF_GSSD = OFF_GATT + D_MODEL
OFF_Z = OFF_GSSD + D_MODEL
OFF_XBC = OFF_Z + D_INNER
OFF_DT = OFF_XBC + CONV_CH
N_IN = OFF_DT + 2 * SSD_HEADS
IN_TILE = 768
N_IN_PAD = ((N_IN + IN_TILE - 1) // IN_TILE) * IN_TILE

LANES = 128
SUBLANES = 8
VMEM_LIMIT = 56 * 1024 * 1024
NEG_BIG = -1e30
LOG2E = 1.4426950408889634

F32 = jnp.float32
BF16 = jnp.bfloat16


def _params(*sem):
    return pltpu.CompilerParams(dimension_semantics=sem, vmem_limit_bytes=VMEM_LIMIT)


def _sigmoid(x):
    return 1.0 / (1.0 + jnp.exp(-x))


def _silu(x):
    h = 0.5 * x
    return h + h * jnp.tanh(h)


def _dot(a, b):
    return jnp.dot(a, b, preferred_element_type=F32)


NORM_ROWS = 128


def _norm_into(x_ref, g_ref, h_ref):
    g = g_ref[...]

    def body(i, c):
        r = pl.ds(pl.multiple_of(i * NORM_ROWS, NORM_ROWS), NORM_ROWS)
        x = x_ref[r, :]
        ms = jnp.mean(x * x, axis=-1, keepdims=True)
        h_ref[r, :] = ((x * lax.rsqrt(ms + EPS)) * g).astype(BF16)
        return c

    lax.fori_loop(0, x_ref.shape[0] // NORM_ROWS, body, 0)


def _norm_matmul_kernel(x_ref, g_ref, w_ref, o_ref, h_ref):
    @pl.when(pl.program_id(1) == 0)
    def _():
        _norm_into(x_ref, g_ref, h_ref)

    o_ref[...] = _dot(h_ref[...], w_ref[...]).astype(o_ref.dtype)


def norm_matmul(x, g, w, layer, *, tm, tn, out_dtype):
    t, k = x.shape
    n = w.shape[2]
    return pl.pallas_call(
        _norm_matmul_kernel,
        grid=(t // tm, n // tn),
        in_specs=[
            pl.BlockSpec((tm, k), lambda i, j: (i, 0)),
            pl.BlockSpec((1, k), lambda i, j: (0, 0)),
            pl.BlockSpec((None, k, tn), lambda i, j: (layer, 0, j)),
        ],
        out_specs=pl.BlockSpec((tm, tn), lambda i, j: (i, j)),
        out_shape=jax.ShapeDtypeStruct((t, n), out_dtype),
        scratch_shapes=[pltpu.VMEM((tm, k), BF16)],
        compiler_params=_params("parallel", "arbitrary"),
        name="norm_in_proj",
    )(x, g, w)


def _norm_swiglu_kernel(x_ref, g_ref, wg_ref, wu_ref, o_ref, h_ref):
    @pl.when(pl.program_id(1) == 0)
    def _():
        _norm_into(x_ref, g_ref, h_ref)

    h = h_ref[...]
    gt = _dot(h, wg_ref[...])
    up = _dot(h, wu_ref[...])
    o_ref[...] = (_silu(gt) * up).astype(o_ref.dtype)


def norm_swiglu(x, g, w_gate_up, layer, *, tm, tn):
    t, k = x.shape
    f = w_gate_up.shape[2] // 2
    nf = f // tn
    return pl.pallas_call(
        _norm_swiglu_kernel,
        grid=(t // tm, nf),
        in_specs=[
            pl.BlockSpec((tm, k), lambda i, j: (i, 0)),
            pl.BlockSpec((1, k), lambda i, j: (0, 0)),
            pl.BlockSpec((None, k, tn), lambda i, j: (layer, 0, j)),
            pl.BlockSpec((None, k, tn), lambda i, j: (layer, 0, j + nf)),
        ],
        out_specs=pl.BlockSpec((tm, tn), lambda i, j: (i, j)),
        out_shape=jax.ShapeDtypeStruct((t, f), BF16),
        scratch_shapes=[pltpu.VMEM((tm, k), BF16)],
        compiler_params=_params("parallel", "arbitrary"),
        name="norm_swiglu",
    )(x, g, w_gate_up, w_gate_up)


def _resid_matmul_kernel(a_ref, w_ref, x_ref, o_ref):
    o_ref[...] = x_ref[...] + _dot(a_ref[...], w_ref[...])


def resid_matmul(a, w, layer, x, *, tm, tn):
    t, k = a.shape
    n = w.shape[2]
    return pl.pallas_call(
        _resid_matmul_kernel,
        grid=(t // tm, n // tn),
        in_specs=[
            pl.BlockSpec((tm, k), lambda i, j: (i, 0)),
            pl.BlockSpec((None, k, tn), lambda i, j: (layer, 0, j)),
            pl.BlockSpec((tm, tn), lambda i, j: (i, j)),
        ],
        out_specs=pl.BlockSpec((tm, tn), lambda i, j: (i, j)),
        out_shape=jax.ShapeDtypeStruct((t, n), F32),
        compiler_params=_params("parallel", "arbitrary"),
        name="resid_matmul",
    )(a, w, x)


def _mix_kernel(attn_ref, ssd_ref, wa_ref, ws_ref, ga_ref, gs_ref, o_ref):
    a = _dot(attn_ref[...], wa_ref[...])
    s = _dot(ssd_ref[...], ws_ref[...])
    o_ref[...] = (_sigmoid(ga_ref[...]) * a + _sigmoid(gs_ref[...]) * s).astype(o_ref.dtype)


def gated_mix(attn, ssd, wa, ws, layer, proj, *, tm, tn):
    t = attn.shape[0]
    n = wa.shape[2]
    return pl.pallas_call(
        _mix_kernel,
        grid=(t // tm, n // tn),
        in_specs=[
            pl.BlockSpec((tm, attn.shape[1]), lambda i, j: (i, 0)),
            pl.BlockSpec((tm, ssd.shape[1]), lambda i, j: (i, 0)),
            pl.BlockSpec((None, wa.shape[1], tn), lambda i, j: (layer, 0, j)),
            pl.BlockSpec((None, ws.shape[1], tn), lambda i, j: (layer, 0, j)),
            pl.BlockSpec((tm, tn), lambda i, j: (i, j + OFF_GATT // tn)),
            pl.BlockSpec((tm, tn), lambda i, j: (i, j + OFF_GSSD // tn)),
        ],
        out_specs=pl.BlockSpec((tm, tn), lambda i, j: (i, j)),
        out_shape=jax.ShapeDtypeStruct((t, n), BF16),
        compiler_params=_params("parallel", "arbitrary"),
        name="gated_mix",
    )(attn, ssd, wa, ws, proj, proj)


def _rmsnorm_kernel(x_ref, g_ref, o_ref):
    x = x_ref[...]
    ms = jnp.mean(x * x, axis=-1, keepdims=True)
    o_ref[...] = (x * lax.rsqrt(ms + EPS)) * g_ref[...]


def rmsnorm(x, g, *, tm, row0, nrows):
    k = x.shape[1]
    off = row0 // tm
    return pl.pallas_call(
        _rmsnorm_kernel,
        grid=(nrows // tm,),
        in_specs=[pl.BlockSpec((tm, k), lambda i: (i + off, 0)), pl.BlockSpec((1, k), lambda i: (0, 0))],
        out_specs=pl.BlockSpec((tm, k), lambda i: (i, 0)),
        out_shape=jax.ShapeDtypeStruct((nrows, k), F32),
        compiler_params=_params("parallel"),
        name="final_rmsnorm",
    )(x, g)


def _rope(x, cc, ss):
    return x * cc + pltpu.roll(x, HEAD_DIM // 2, 1) * ss


def _attn_kernel(sink_ref, q_ref, kp_ref, kc_ref, kn_ref, vp_ref, vc_ref, vn_ref, cc_ref, ss_ref, o_ref):
    n = pl.program_id(1)
    nb = pl.num_programs(1)
    pos_c = pl.multiple_of(n * BLOCK, BLOCK)
    pos_p = pl.multiple_of(jnp.maximum(n - 1, 0) * BLOCK, BLOCK)
    pos_n = pl.multiple_of(jnp.minimum(n + 1, nb - 1) * BLOCK, BLOCK)
    cc_c, ss_c = cc_ref[pl.ds(pos_c, BLOCK), :], ss_ref[pl.ds(pos_c, BLOCK), :]
    cc_k = jnp.concatenate([cc_ref[pl.ds(pos_p, BLOCK), :], cc_c, cc_ref[pl.ds(pos_n, BLOCK), :]], axis=0)
    ss_k = jnp.concatenate([ss_ref[pl.ds(pos_p, BLOCK), :], ss_c, ss_ref[pl.ds(pos_n, BLOCK), :]], axis=0)
    cc_q = jnp.concatenate([cc_c] * Q_PER_KV, axis=0)
    ss_q = jnp.concatenate([ss_c] * Q_PER_KV, axis=0)

    rows = Q_PER_KV * BLOCK
    t = lax.broadcasted_iota(jnp.int32, (rows, BLOCK), 0) % BLOCK
    sk = lax.broadcasted_iota(jnp.int32, (rows, BLOCK), 1)
    valid_prev = sk >= t + jnp.where(n > 0, 0, BLOCK)
    valid_next = sk <= t - jnp.where(n < nb - 1, 0, BLOCK)
    scale2 = HEAD_DIM ** -0.5 * LOG2E

    for hk in range(N_KV_HEADS):
        c = slice(hk * HEAD_DIM, (hk + 1) * HEAD_DIM)
        kwin = jnp.concatenate([kp_ref[0, :, c], kc_ref[0, :, c], kn_ref[0, :, c]], axis=0)
        vwin = jnp.concatenate([vp_ref[0, :, c], vc_ref[0, :, c], vn_ref[0, :, c]], axis=0).astype(BF16)
        kr = _rope(kwin, cc_k, ss_k).astype(BF16)
        qg = jnp.concatenate(
            [q_ref[0, :, (hk * Q_PER_KV + g) * HEAD_DIM:(hk * Q_PER_KV + g + 1) * HEAD_DIM] for g in range(Q_PER_KV)],
            axis=0,
        )
        qr = _rope(qg, cc_q, ss_q).astype(BF16)
        s = lax.dot_general(qr, kr, (((1,), (1,)), ((), ())), preferred_element_type=F32) * scale2
        s = jnp.concatenate([jnp.where(valid_prev, s[:, :BLOCK], NEG_BIG), s[:, BLOCK:2 * BLOCK],
                             jnp.where(valid_next, s[:, 2 * BLOCK:], NEG_BIG)], axis=1)
        ps, dens = [], []
        for g in range(Q_PER_KV):
            sg = s[g * BLOCK:(g + 1) * BLOCK]
            sink2 = sink_ref[hk * Q_PER_KV + g] * LOG2E
            m = jnp.maximum(jnp.max(sg, axis=-1, keepdims=True), sink2)
            p = jnp.exp2(sg - m)
            dens.append(jnp.sum(p, axis=-1, keepdims=True) + jnp.exp2(sink2 - m))
            ps.append(p.astype(BF16))
        pv = _dot(jnp.concatenate(ps, axis=0), vwin)
        for g in range(Q_PER_KV):
            h = hk * Q_PER_KV + g
            o = pv[g * BLOCK:(g + 1) * BLOCK] / dens[g]
            o_ref[0, :, h * HEAD_DIM:(h + 1) * HEAD_DIM] = o.astype(o_ref.dtype)


def windowed_attention(proj3, sink, cc, ss):
    nseq, l, _ = proj3.shape
    nb = l // BLOCK
    kcol, vcol = OFF_K // KV_W, OFF_V // KV_W
    prev = lambda n: jnp.maximum(n - 1, 0)
    nxt = lambda n: jnp.minimum(n + 1, nb - 1)
    kv = lambda rowf, col: pl.BlockSpec((1, BLOCK, KV_W), lambda b, n: (b, rowf(n), col))
    ident = lambda n: n
    return pl.pallas_call(
        _attn_kernel,
        grid=(nseq, nb),
        in_specs=[
            pl.BlockSpec(memory_space=pltpu.SMEM),
            pl.BlockSpec((1, BLOCK, ATTN_W), lambda b, n: (b, n, OFF_Q // ATTN_W)),
            kv(prev, kcol), kv(ident, kcol), kv(nxt, kcol),
            kv(prev, vcol), kv(ident, vcol), kv(nxt, vcol),
            pl.BlockSpec((l, HEAD_DIM), lambda b, n: (0, 0)),
            pl.BlockSpec((l, HEAD_DIM), lambda b, n: (0, 0)),
        ],
        out_specs=pl.BlockSpec((1, BLOCK, ATTN_W), lambda b, n: (b, n, 0)),
        out_shape=jax.ShapeDtypeStruct((nseq, l, ATTN_W), BF16),
        compiler_params=_params("parallel", "arbitrary"),
        name="windowed_attention",
    )(sink, proj3, proj3, proj3, proj3, proj3, proj3, proj3, cc, ss)


HALO = SUBLANES
PAD = D_CONV // 2
CONV_PHASES = 4
CONV_SUB = 64
CONV_BLK = CONV_PHASES * CONV_SUB


def _conv_kernel(x_ref, w_ref, b_ref, o_ref, edge_ref, *res_ref):
    rows = x_ref.shape[1]
    nblk = rows // CONV_BLK
    w = [w_ref[j:j + 1, :] for j in range(D_CONV)]
    bias = b_ref[...]

    def emit(tap, out_base):
        for r in range(CONV_PHASES):
            acc = bias
            for j in range(D_CONV):
                acc = acc + tap(r + j - PAD) * w[j]
            y = _silu(acc)
            if res_ref:
                res_ref[0][pl.ds(out_base + r, CONV_SUB, stride=CONV_PHASES), :] = y
            else:
                o_ref[0, pl.ds(out_base + r, CONV_SUB, stride=CONV_PHASES), :] = y

    def body(i, c):
        base = i * CONV_BLK
        emit(lambda off: x_ref[0, pl.ds(base + off, CONV_SUB, stride=CONV_PHASES), :], base)
        return c

    lax.fori_loop(1, nblk - 1, body, 0)

    zeros = jnp.zeros((HALO, LANES), F32)
    edge_tap = lambda slot: (lambda off: edge_ref[slot, pl.ds(HALO + off, CONV_SUB, stride=CONV_PHASES), :])
    edge_ref[0, 0:HALO, :] = zeros
    if nblk == 1:
        edge_ref[0, HALO:HALO + rows, :] = x_ref[0]
        edge_ref[0, HALO + rows:, :] = zeros
        emit(edge_tap(0), 0)
    else:
        edge_ref[0, HALO:, :] = x_ref[0, 0:CONV_BLK + HALO, :]
        edge_ref[1, 0:HALO + CONV_BLK, :] = x_ref[0, rows - CONV_BLK - HALO:rows, :]
        edge_ref[1, HALO + CONV_BLK:, :] = zeros
        emit(edge_tap(0), 0)
        emit(edge_tap(1), rows - CONV_BLK)
    if res_ref:
        o_ref[0] = res_ref[0][...].astype(o_ref.dtype)


def conv_silu(proj3, conv_w, conv_b, *, ch_off, n_ch, out_dtype):
    nseq, l, _ = proj3.shape
    col0 = (OFF_XBC + ch_off) // LANES
    wcol0 = ch_off // LANES
    scratch = [pltpu.VMEM((2, CONV_BLK + 2 * HALO, LANES), F32)]
    if out_dtype != F32:
        scratch.append(pltpu.VMEM((l, LANES), F32))
    return pl.pallas_call(
        _conv_kernel,
        grid=(nseq, n_ch // LANES),
        in_specs=[
            pl.BlockSpec((1, l, LANES), lambda b, c: (b, 0, col0 + c)),
            pl.BlockSpec((D_CONV, LANES), lambda b, c: (0, wcol0 + c)),
            pl.BlockSpec((1, LANES), lambda b, c: (0, wcol0 + c)),
        ],
        out_specs=pl.BlockSpec((1, l, LANES), lambda b, c: (b, 0, c)),
        out_shape=jax.ShapeDtypeStruct((nseq, l, n_ch), out_dtype),
        scratch_shapes=scratch,
        compiler_params=_params("parallel", "parallel"),
        name="conv_silu",
    )(proj3, conv_w, conv_b)


DT_CHUNKS = 8
PAIRS = HEADS_PER_GROUP // 2
DH = 2 * SSD_HEADS
COL_AF, COL_AB, COL_GF, COL_GB = (k * HEADS_PER_GROUP for k in range(4))
COL_USED = 4 * HEADS_PER_GROUP
GROUP_LANES = SSD_HEADS // 2


def _split3(a):
    hi = a.astype(BF16)
    r = a - hi.astype(F32)
    mid = r.astype(BF16)
    lo = (r - mid.astype(F32)).astype(BF16)
    return hi, mid, lo


def _tri_sum(tri, a):
    hi, mid, lo = _split3(a)
    return _dot(tri, hi) + _dot(tri, mid) + _dot(tri, lo)


def _pair_rows(v, row_lo, lane_even):
    swap = lambda u: jnp.concatenate([u[CHUNK // 2:], u[:CHUNK // 2]], axis=0)
    from_prev_lane = swap(pltpu.roll(v, 1, 1))
    from_next_lane = swap(pltpu.roll(v, DH - 1, 1))
    w = jnp.where(row_lo, jnp.where(lane_even, v, from_prev_lane), jnp.where(lane_even, from_next_lane, v))
    return w.T


def _dt_kernel(raw_ref, bias_ref, alog_ref, col_ref, rowg_ref, rowdt_ref, tot_ref):
    neg_a2 = -jnp.exp(alog_ref[...]) * LOG2E
    bias = bias_ref[...]
    r = lax.broadcasted_iota(jnp.int32, (CHUNK, CHUNK), 0)
    c = lax.broadcasted_iota(jnp.int32, (CHUNK, CHUNK), 1)
    lower = (r >= c).astype(BF16)
    upper = (r <= c).astype(BF16)
    fwd_lane = c < SSD_HEADS
    row_lo = r < CHUNK // 2
    lane_even = (c & 1) == 0
    src0 = ((c >> 3) & 1) * SSD_HEADS + (c & (HEADS_PER_GROUP - 1))
    low_groups = (c & GROUP_LANES) == 0
    for ci in range(raw_ref.shape[0] // CHUNK):
        rows = slice(ci * CHUNK, (ci + 1) * CHUNK)
        raw = raw_ref[rows, :] + bias
        dt = jnp.maximum(raw, 0.0) + jnp.log1p(jnp.exp(-jnp.abs(raw)))
        a2 = dt * neg_a2
        pre = _tri_sum(lower, a2)
        suf = _tri_sum(upper, a2)
        acum2 = jnp.where(fwd_lane, pre, suf)
        g = jnp.log2(dt) - acum2
        tot_ref[ci] = jnp.where(fwd_lane[0:1], pre[CHUNK - 1:CHUNK], suf[0:1])
        rowg_ref[ci] = _pair_rows(g, row_lo, lane_even)
        rowdt_ref[ci] = _pair_rows(dt, row_lo, lane_even)
        sources = (jnp.where(low_groups, acum2, pltpu.roll(g, GROUP_LANES, 1)),
                   jnp.where(low_groups, pltpu.roll(g, DH - GROUP_LANES, 1), acum2))
        for grp in range(SSD_GROUPS):
            hi = grp >= SSD_GROUPS // 2
            idx = src0 + grp * HEADS_PER_GROUP + jnp.where(c < COL_GF, 0, -GROUP_LANES if hi else GROUP_LANES)
            packed = jnp.take_along_axis(sources[hi], idx, axis=1)
            col_ref[grp, rows, :] = jnp.where(c < COL_USED, packed, 0.0)


def dt_prepare(proj, dt_bias, a_log):
    t = proj.shape[0]
    rows = DT_CHUNKS * CHUNK
    if t % rows:
        rows = CHUNK
    cps = rows // CHUNK
    nchunks = t // CHUNK
    col = OFF_DT // DH
    return pl.pallas_call(
        _dt_kernel,
        grid=(t // rows,),
        in_specs=[
            pl.BlockSpec((rows, DH), lambda i: (i, col)),
            pl.BlockSpec((1, DH), lambda i: (0, 0)),
            pl.BlockSpec((1, DH), lambda i: (0, 0)),
        ],
        out_specs=[
            pl.BlockSpec((SSD_GROUPS, rows, LANES), lambda i: (0, i, 0)),
            pl.BlockSpec((cps, DH, CHUNK), lambda i: (i, 0, 0)),
            pl.BlockSpec((cps, DH, CHUNK), lambda i: (i, 0, 0)),
            pl.BlockSpec((cps, 1, DH), lambda i: (i, 0, 0)),
        ],
        out_shape=[
            jax.ShapeDtypeStruct((SSD_GROUPS, t, LANES), F32),
            jax.ShapeDtypeStruct((nchunks, DH, CHUNK), F32),
            jax.ShapeDtypeStruct((nchunks, DH, CHUNK), F32),
            jax.ShapeDtypeStruct((nchunks, 1, DH), F32),
        ],
        compiler_params=_params("parallel"),
        name="dt_prepare",
    )(proj, dt_bias, a_log)


SSD_CPS = 16
HALF = CHUNK // 2


def _expand(colv, base, lane_lo):
    tiles = []
    for p in range(PAIRS):
        idx = jnp.where(lane_lo, base + 2 * p, base + 2 * p + 1)
        tiles.append(jnp.take_along_axis(colv, idx, axis=1))
    return jnp.concatenate(tiles, axis=1)


def _ssd_kernel(x_ref, b_ref, c_ref, z_ref, col_ref, rowgf_ref, rowgb_ref, rowdtf_ref,
                tot_ref, skip_ref, ng_ref, o_ref, sf_ref, sb_ref, sb_all_ref):
    ph = pl.program_id(2)
    cs = pl.program_id(3)
    ncs = pl.num_programs(3)
    cps = x_ref.shape[1] // CHUNK
    lane = lax.broadcasted_iota(jnp.int32, (CHUNK, LANES), 1)
    lane_lo = lane < SSD_HEAD_DIM

    def state_update(s_ref, xs, bc, g_x, tot_row):
        xw = (xs * jnp.exp2(tot_row + g_x)).astype(BF16)
        st = lax.dot_general(bc, xw, (((0,), (0,)), ((), ())), preferred_element_type=F32)
        s_ref[...] = s_ref[...] * jnp.exp2(tot_row) + st

    @pl.when(ph == 0)
    def _backward_states():
        @pl.when(cs == 0)
        def _():
            sb_ref[...] = jnp.zeros_like(sb_ref)

        for k in range(cps):
            ci = cps - 1 - k
            rows = slice(ci * CHUNK, (ci + 1) * CHUNK)
            chunk = (ncs - 1 - cs) * cps + ci
            sb_all_ref[chunk] = sb_ref[...].astype(BF16)
            gb = _expand(col_ref[0, 0, rows, :], COL_GB, lane_lo)
            state_update(sb_ref, x_ref[0, rows, :], b_ref[0, rows, :], gb, tot_ref[0, ci, 1])

    @pl.when(ph == 1)
    def _forward_and_output():
        @pl.when(cs == 0)
        def _():
            sf_ref[...] = jnp.zeros_like(sf_ref)

        li = lax.broadcasted_iota(jnp.int32, (HALF, LANES), 0)
        s_in_half = lax.broadcasted_iota(jnp.int32, (HALF, LANES), 1) & (HALF - 1)
        fwd = li > s_in_half
        diag = li == s_in_half
        for ci in range(cps):
            rows = slice(ci * CHUNK, (ci + 1) * CHUNK)
            chunk = cs * cps + ci
            xs = x_ref[0, rows, :]
            xs16 = xs.astype(BF16)
            bc = b_ref[0, rows, :]
            cc = c_ref[0, rows, :]
            cb = lax.dot_general(cc, bc, (((1,), (1,)), ((), ())), preferred_element_type=F32)
            cb_rot = pltpu.roll(cb, HALF, 1)
            cbs = [jnp.where(lane_lo, cb, cb_rot), jnp.where(lane_lo, cb_rot, cb)]
            colv = col_ref[0, 0, rows, :]
            af = _expand(colv, COL_AF, lane_lo)
            ab = _expand(colv, COL_AB, lane_lo)
            gf = _expand(colv, COL_GF, lane_lo)

            ydiag = []
            for p in range(PAIRS):
                afp = af[:, p * LANES:(p + 1) * LANES]
                abp = ab[:, p * LANES:(p + 1) * LANES]
                tiles = []
                for half in (0, 1):
                    rr = slice(2 * p + half, 2 * p + half + 1)
                    rgf, rgb = rowgf_ref[0, ci, rr, :], rowgb_ref[0, ci, rr, :]
                    same = slice(half * HALF, (half + 1) * HALF)
                    e_same = jnp.exp2(jnp.where(fwd, afp[same] + rgf, abp[same] + rgb))
                    e_same = e_same + jnp.where(diag, rowdtf_ref[0, ci, rr, :], 0.0)
                    if half == 0:
                        e = jnp.concatenate([e_same, jnp.exp2(afp[HALF:] + rgf)], axis=0)
                    else:
                        e = jnp.concatenate([jnp.exp2(abp[:HALF] + rgb), e_same], axis=0)
                    tiles.append((cbs[half] * e).astype(BF16))
                xp = xs16[:, p * LANES:(p + 1) * LANES]
                zero = jnp.zeros_like(xp)
                x_lo = jnp.where(lane_lo, xp, zero)
                x_hi = jnp.where(lane_lo, zero, xp)
                rhs = jnp.concatenate([x_lo[:HALF], x_hi[:HALF], x_lo[HALF:], x_hi[HALF:]], axis=0)
                ydiag.append(_dot(jnp.concatenate(tiles, axis=1), rhs))
            y = jnp.concatenate(ydiag, axis=1)

            y = y + _dot(cc, sf_ref[...].astype(BF16)) * jnp.exp2(af)
            y = y + _dot(cc, sb_all_ref[chunk]) * jnp.exp2(ab)
            y = y + xs * skip_ref[...]

            state_update(sf_ref, xs, bc, gf, tot_ref[0, ci, 0])

            z = z_ref[0, rows, :]
            y = y * _silu(z)
            ms = jnp.mean(y * y, axis=-1, keepdims=True)
            o_ref[0, rows, :] = ((y * lax.rsqrt(ms + EPS)) * ng_ref[...]).astype(o_ref.dtype)


def ssd_scan(xs, bc, proj3, colpack, rowg, rowdt, tot_x, skip_x, norm_g):
    nseq, l, _ = xs.shape
    nchunks = l // CHUNK
    cps = SSD_CPS if nchunks % SSD_CPS == 0 else 1
    ncs = nchunks // cps
    rows = cps * CHUNK
    blk = lambda ph, cs: jnp.where(ph == 0, ncs - 1 - cs, cs)
    fwd_only = lambda ph, cs: jnp.where(ph == 0, 0, cs)
    zcol = OFF_Z // GROUP_W
    row_spec = lambda d: pl.BlockSpec(
        (1, cps, HEADS_PER_GROUP, CHUNK), lambda b, g, ph, cs: (b, blk(ph, cs), d * SSD_GROUPS + g, 0))
    return pl.pallas_call(
        _ssd_kernel,
        grid=(nseq, SSD_GROUPS, 2, ncs),
        in_specs=[
            pl.BlockSpec((1, rows, GROUP_W), lambda b, g, ph, cs: (b, blk(ph, cs), g)),
            pl.BlockSpec((1, rows, D_STATE), lambda b, g, ph, cs: (b, blk(ph, cs), g)),
            pl.BlockSpec((1, rows, D_STATE), lambda b, g, ph, cs: (b, fwd_only(ph, cs), SSD_GROUPS + g)),
            pl.BlockSpec((1, rows, GROUP_W), lambda b, g, ph, cs: (b, fwd_only(ph, cs), zcol + g)),
            pl.BlockSpec((1, 1, rows, LANES), lambda b, g, ph, cs: (g, b, blk(ph, cs), 0)),
            row_spec(0), row_spec(1), row_spec(0),
            pl.BlockSpec((1, cps, 2, 1, GROUP_W), lambda b, g, ph, cs: (b, blk(ph, cs), 0, 0, g)),
            pl.BlockSpec((1, GROUP_W), lambda b, g, ph, cs: (0, g)),
            pl.BlockSpec((1, GROUP_W), lambda b, g, ph, cs: (0, g)),
        ],
        out_specs=pl.BlockSpec((1, rows, GROUP_W), lambda b, g, ph, cs: (b, fwd_only(ph, cs), g)),
        out_shape=jax.ShapeDtypeStruct((nseq, l, D_INNER), BF16),
        scratch_shapes=[
            pltpu.VMEM((D_STATE, GROUP_W), F32),
            pltpu.VMEM((D_STATE, GROUP_W), F32),
            pltpu.VMEM((nchunks, D_STATE, GROUP_W), BF16),
        ],
        compiler_params=_params("parallel", "parallel", "arbitrary", "arbitrary"),
        name="ssd_scan",
    )(xs, bc, bc, proj3, colpack, rowg, rowg, rowdt, tot_x, skip_x, norm_g)


def _rope_tables(l):
    half = HEAD_DIM // 2
    inv_freq = ROPE_THETA ** (-jnp.arange(half, dtype=F32) * 2.0 / HEAD_DIM)
    ang = jnp.arange(l, dtype=F32)[:, None] * inv_freq[None, :]
    cos, sin = jnp.cos(ang), jnp.sin(ang)
    return jnp.concatenate([cos, cos], axis=-1), jnp.concatenate([-sin, sin], axis=-1)


def _row_tile(t, want):
    while t % want:
        want //= 2
    return want


def _layer(layer, x, nseq, l, cc, ss, w_in, conv_w, conv_b, a_log, dt_bias, skip_x, ssd_norm, attn_sink,
           w_out_attn, w_out_ssd, w_out, norm_mix, norm_ffn, w_gate_up, w_down):
    t = nseq * l
    nchunks = l // CHUNK
    proj = norm_matmul(x, norm_mix, w_in, layer, tm=_row_tile(t, 1024), tn=IN_TILE, out_dtype=F32)
    proj3 = proj.reshape(nseq, l, N_IN_PAD)

    attn = windowed_attention(proj3, attn_sink, cc, ss).reshape(t, ATTN_W)

    xs = conv_silu(proj3, conv_w, conv_b, ch_off=0, n_ch=D_INNER, out_dtype=F32)
    bcm = conv_silu(proj3, conv_w, conv_b, ch_off=D_INNER, n_ch=CONV_CH - D_INNER, out_dtype=BF16)

    colpack, rowg, rowdt, tot = dt_prepare(proj, dt_bias, a_log)
    tot_x = jnp.repeat(tot.reshape(nseq, nchunks, 2, SSD_HEADS), SSD_HEAD_DIM, axis=-1)
    tot_x = tot_x.reshape(nseq, nchunks, 2, 1, D_INNER)
    colpack = colpack.reshape(SSD_GROUPS, nseq, l, LANES)
    rowg = rowg.reshape(nseq, nchunks, DH, CHUNK)
    rowdt = rowdt.reshape(nseq, nchunks, DH, CHUNK)
    ssd = ssd_scan(xs, bcm, proj3, colpack, rowg, rowdt, tot_x, skip_x, ssd_norm).reshape(t, D_INNER)

    mix = gated_mix(attn, ssd, w_out_attn, w_out_ssd, layer, proj, tm=_row_tile(t, 1024), tn=256)
    x = resid_matmul(mix, w_out, layer, x, tm=_row_tile(t, 1024), tn=1024)
    act = norm_swiglu(x, norm_ffn, w_gate_up, layer, tm=_row_tile(t, 1024), tn=512)
    x = resid_matmul(act, w_down, layer, x, tm=_row_tile(t, 1024), tn=512)
    return x


def _trunk(x, w_in, conv_w, conv_b, a_log, dt_bias, d_skip, ssd_norm, attn_sink,
           w_out_attn, w_out_ssd, w_out, norm_mix, norm_ffn, w_gate_up, w_down):
    nseq, l, _ = x.shape
    x = x.reshape(nseq * l, D_MODEL)
    cc, ss = _rope_tables(l)
    w_in16 = jnp.pad(w_in.astype(BF16), ((0, 0), (0, 0), (0, N_IN_PAD - N_IN)))
    mats = [w.astype(BF16) for w in (w_out_attn, w_out_ssd, w_out)]
    ffn = [w.astype(BF16) for w in (w_gate_up, w_down)]
    for i in range(DEPTH):
        x = _layer(
            i, x, nseq, l, cc, ss, w_in16, conv_w[i], conv_b[i][None, :],
            a_log[i].reshape(1, -1), dt_bias[i].reshape(1, -1), jnp.repeat(d_skip[i], SSD_HEAD_DIM)[None, :],
            ssd_norm[i][None, :], attn_sink[i], *mats, norm_mix[i][None, :], norm_ffn[i][None, :], *ffn)
    return x


def kernel(x_prompt, x_sample, w_in, conv_w, conv_b, a_log, dt_bias, d_skip, ssd_norm, attn_sink,
           w_out_attn, w_out_ssd, w_out, norm_mix, norm_ffn, w_gate_up, w_down, final_norm):
    weights = (w_in, conv_w, conv_b, a_log, dt_bias, d_skip, ssd_norm, attn_sink,
               w_out_attn, w_out_ssd, w_out, norm_mix, norm_ffn, w_gate_up, w_down)
    g = final_norm[None, :]
    outs = []
    if x_prompt.shape[1] == x_sample.shape[1]:
        l = x_prompt.shape[1]
        x = _trunk(jnp.concatenate([x_prompt, x_sample], axis=0), *weights)
        row0 = 0
        for xin in (x_prompt, x_sample):
            n = xin.shape[0] * l
            outs.append(rmsnorm(x, g, tm=_row_tile(l, 512), row0=row0, nrows=n).reshape(xin.shape))
            row0 += n
    else:
        for xin in (x_prompt, x_sample):
            x = _trunk(xin, *weights)
            outs.append(rmsnorm(x, g, tm=_row_tile(x.shape[0], 512), row0=0, nrows=x.shape[0]).reshape(xin.shape))
    return tuple(outs)
```

```python
import jax
import jax.numpy as jnp
from jax import lax
from jax.experimental import pallas as pl
from jax.experimental.pallas import tpu as pltpu

D_MODEL = 2048
DEPTH = 4
N_HEADS = 16
N_KV_HEADS = 4
HEAD_DIM = 128
Q_PER_KV = N_HEADS // N_KV_HEADS
ATTN_W = N_HEADS * HEAD_DIM
KV_W = N_KV_HEADS * HEAD_DIM
BLOCK = 128
ROPE_THETA = 10000.0
D_INNER = 2 * D_MODEL
SSD_HEAD_DIM = 64
SSD_HEADS = D_INNER // SSD_HEAD_DIM
SSD_GROUPS = 8
HEADS_PER_GROUP = SSD_HEADS // SSD_GROUPS
GROUP_W = D_INNER // SSD_GROUPS
D_STATE = 128
D_CONV = 5
CHUNK = 128
CONV_CH = D_INNER + 2 * SSD_GROUPS * D_STATE
D_FF = ((8 * D_MODEL // 3 + 255) // 256) * 256
EPS = 1e-6

OFF_Q = 0
OFF_K = OFF_Q + ATTN_W
OFF_V = OFF_K + KV_W
OFF_GATT = OFF_V + KV_W
OFF_GSSD = OFF_GATT + D_MODEL
OFF_Z = OFF_GSSD + D_MODEL
OFF_XBC = OFF_Z + D_INNER
OFF_DT = OFF_XBC + CONV_CH
N_IN = OFF_DT + 2 * SSD_HEADS
IN_TILE = 768

LANES = 128
SUBLANES = 8
VMEM_LIMIT = 56 * 1024 * 1024
NEG_BIG = -1e30
LOG2E = 1.4426950408889634

F32 = jnp.float32
BF16 = jnp.bfloat16


def _params(*sem):
    return pltpu.CompilerParams(dimension_semantics=sem, vmem_limit_bytes=VMEM_LIMIT)


def _sigmoid(x):
    return 1.0 / (1.0 + jnp.exp(-x))


def _silu(x):
    h = 0.5 * x
    return h + h * jnp.tanh(h)


def _dot(a, b):
    return jnp.dot(a, b, preferred_element_type=F32)


NORM_ROWS = 128


def _norm_into(x_ref, g_ref, h_ref):
    g = g_ref[...]

    def body(i, c):
        r = pl.ds(pl.multiple_of(i * NORM_ROWS, NORM_ROWS), NORM_ROWS)
        x = x_ref[r, :]
        ms = jnp.mean(x * x, axis=-1, keepdims=True)
        h_ref[r, :] = ((x * lax.rsqrt(ms + EPS)) * g).astype(BF16)
        return c

    lax.fori_loop(0, x_ref.shape[0] // NORM_ROWS, body, 0)


def _norm_matmul_kernel(x_ref, g_ref, w_ref, o_ref, h_ref):
    @pl.when(pl.program_id(1) == 0)
    def _():
        _norm_into(x_ref, g_ref, h_ref)

    o_ref[...] = _dot(h_ref[...], w_ref[...]).astype(o_ref.dtype)


def norm_matmul(x, g, w, layer, *, tm, tn, out_dtype):
    t, k = x.shape
    n = w.shape[2]
    return pl.pallas_call(
        _norm_matmul_kernel,
        grid=(t // tm, pl.cdiv(n, tn)),
        in_specs=[
            pl.BlockSpec((tm, k), lambda i, j: (i, 0)),
            pl.BlockSpec((1, k), lambda i, j: (0, 0)),
            pl.BlockSpec((None, k, tn), lambda i, j: (layer, 0, j)),
        ],
        out_specs=pl.BlockSpec((tm, tn), lambda i, j: (i, j)),
        out_shape=jax.ShapeDtypeStruct((t, n), out_dtype),
        scratch_shapes=[pltpu.VMEM((tm, k), BF16)],
        compiler_params=_params("parallel", "arbitrary"),
        name="norm_in_proj",
    )(x, g, w)


def _norm_swiglu_kernel(x_ref, g_ref, wg_ref, wu_ref, o_ref, h_ref):
    @pl.when(pl.program_id(1) == 0)
    def _():
        _norm_into(x_ref, g_ref, h_ref)

    h = h_ref[...]
    gt = _dot(h, wg_ref[...])
    up = _dot(h, wu_ref[...])
    o_ref[...] = (_silu(gt) * up).astype(o_ref.dtype)


def norm_swiglu(x, g, w_gate_up, layer, *, tm, tn):
    t, k = x.shape
    f = w_gate_up.shape[2] // 2
    nf = f // tn
    return pl.pallas_call(
        _norm_swiglu_kernel,
        grid=(t // tm, nf),
        in_specs=[
            pl.BlockSpec((tm, k), lambda i, j: (i, 0)),
            pl.BlockSpec((1, k), lambda i, j: (0, 0)),
            pl.BlockSpec((None, k, tn), lambda i, j: (layer, 0, j)),
            pl.BlockSpec((None, k, tn), lambda i, j: (layer, 0, j + nf)),
        ],
        out_specs=pl.BlockSpec((tm, tn), lambda i, j: (i, j)),
        out_shape=jax.ShapeDtypeStruct((t, f), BF16),
        scratch_shapes=[pltpu.VMEM((tm, k), BF16)],
        compiler_params=_params("parallel", "arbitrary"),
        name="norm_swiglu",
    )(x, g, w_gate_up, w_gate_up)


def _resid_matmul_kernel(a_ref, w_ref, x_ref, o_ref):
    o_ref[...] = x_ref[...] + _dot(a_ref[...], w_ref[...])


def resid_matmul(a, w, layer, x, *, tm, tn):
    t, k = a.shape
    n = w.shape[2]
    return pl.pallas_call(
        _resid_matmul_kernel,
        grid=(t // tm, n // tn),
        in_specs=[
            pl.BlockSpec((tm, k), lambda i, j: (i, 0)),
            pl.BlockSpec((None, k, tn), lambda i, j: (layer, 0, j)),
            pl.BlockSpec((tm, tn), lambda i, j: (i, j)),
        ],
        out_specs=pl.BlockSpec((tm, tn), lambda i, j: (i, j)),
        out_shape=jax.ShapeDtypeStruct((t, n), F32),
        compiler_params=_params("parallel", "arbitrary"),
        name="resid_matmul",
    )(a, w, x)


def _mix_kernel(attn_ref, ssd_ref, wa_ref, ws_ref, ga_ref, gs_ref, o_ref):
    a = _dot(attn_ref[...], wa_ref[...])
    s = _dot(ssd_ref[...], ws_ref[...])
    o_ref[...] = (_sigmoid(ga_ref[...]) * a + _sigmoid(gs_ref[...]) * s).astype(o_ref.dtype)


def gated_mix(attn, ssd, wa, ws, layer, proj, *, tm, tn):
    t = attn.shape[0]
    n = wa.shape[2]
    return pl.pallas_call(
        _mix_kernel,
        grid=(t // tm, n // tn),
        in_specs=[
            pl.BlockSpec((tm, attn.shape[1]), lambda i, j: (i, 0)),
            pl.BlockSpec((tm, ssd.shape[1]), lambda i, j: (i, 0)),
            pl.BlockSpec((None, wa.shape[1], tn), lambda i, j: (layer, 0, j)),
            pl.BlockSpec((None, ws.shape[1], tn), lambda i, j: (layer, 0, j)),
            pl.BlockSpec((tm, tn), lambda i, j: (i, j + OFF_GATT // tn)),
            pl.BlockSpec((tm, tn), lambda i, j: (i, j + OFF_GSSD // tn)),
        ],
        out_specs=pl.BlockSpec((tm, tn), lambda i, j: (i, j)),
        out_shape=jax.ShapeDtypeStruct((t, n), BF16),
        compiler_params=_params("parallel", "arbitrary"),
        name="gated_mix",
    )(attn, ssd, wa, ws, proj, proj)


def _rmsnorm_kernel(x_ref, g_ref, o_ref):
    x = x_ref[...]
    ms = jnp.mean(x * x, axis=-1, keepdims=True)
    o_ref[...] = (x * lax.rsqrt(ms + EPS)) * g_ref[...]


def rmsnorm(x, g, *, tm, row0, nrows):
    k = x.shape[1]
    off = row0 // tm
    return pl.pallas_call(
        _rmsnorm_kernel,
        grid=(nrows // tm,),
        in_specs=[pl.BlockSpec((tm, k), lambda i: (i + off, 0)), pl.BlockSpec((1, k), lambda i: (0, 0))],
        out_specs=pl.BlockSpec((tm, k), lambda i: (i, 0)),
        out_shape=jax.ShapeDtypeStruct((nrows, k), F32),
        compiler_params=_params("parallel"),
        name="final_rmsnorm",
    )(x, g)


def _rope(x, cc, ss):
    return x * cc + pltpu.roll(x, HEAD_DIM // 2, 1) * ss


def _attn_kernel(sink_ref, q_ref, kp_ref, kc_ref, kn_ref, vp_ref, vc_ref, vn_ref, cc_ref, ss_ref, o_ref):
    n = pl.program_id(1)
    nb = pl.num_programs(1)
    pos_c = pl.multiple_of(n * BLOCK, BLOCK)
    pos_p = pl.multiple_of(jnp.maximum(n - 1, 0) * BLOCK, BLOCK)
    pos_n = pl.multiple_of(jnp.minimum(n + 1, nb - 1) * BLOCK, BLOCK)
    cc_c, ss_c = cc_ref[pl.ds(pos_c, BLOCK), :], ss_ref[pl.ds(pos_c, BLOCK), :]
    cc_k = jnp.concatenate([cc_ref[pl.ds(pos_p, BLOCK), :], cc_c, cc_ref[pl.ds(pos_n, BLOCK), :]], axis=0)
    ss_k = jnp.concatenate([ss_ref[pl.ds(pos_p, BLOCK), :], ss_c, ss_ref[pl.ds(pos_n, BLOCK), :]], axis=0)
    cc_q = jnp.concatenate([cc_c] * Q_PER_KV, axis=0)
    ss_q = jnp.concatenate([ss_c] * Q_PER_KV, axis=0)

    rows = Q_PER_KV * BLOCK
    t = lax.broadcasted_iota(jnp.int32, (rows, BLOCK), 0) % BLOCK
    sk = lax.broadcasted_iota(jnp.int32, (rows, BLOCK), 1)
    valid_prev = sk >= t + jnp.where(n > 0, 0, BLOCK)
    valid_next = sk <= t - jnp.where(n < nb - 1, 0, BLOCK)
    scale2 = HEAD_DIM ** -0.5 * LOG2E

    for hk in range(N_KV_HEADS):
        c = slice(hk * HEAD_DIM, (hk + 1) * HEAD_DIM)
        kwin = jnp.concatenate([kp_ref[0, :, c], kc_ref[0, :, c], kn_ref[0, :, c]], axis=0)
        vwin = jnp.concatenate([vp_ref[0, :, c], vc_ref[0, :, c], vn_ref[0, :, c]], axis=0).astype(BF16)
        kr = _rope(kwin, cc_k, ss_k).astype(BF16)
        qg = jnp.concatenate(
            [q_ref[0, :, (hk * Q_PER_KV + g) * HEAD_DIM:(hk * Q_PER_KV + g + 1) * HEAD_DIM] for g in range(Q_PER_KV)],
            axis=0,
        )
        qr = _rope(qg, cc_q, ss_q).astype(BF16)
        s = lax.dot_general(qr, kr, (((1,), (1,)), ((), ())), preferred_element_type=F32) * scale2
        s = jnp.concatenate([jnp.where(valid_prev, s[:, :BLOCK], NEG_BIG), s[:, BLOCK:2 * BLOCK],
                             jnp.where(valid_next, s[:, 2 * BLOCK:], NEG_BIG)], axis=1)
        ps, dens = [], []
        for g in range(Q_PER_KV):
            sg = s[g * BLOCK:(g + 1) * BLOCK]
            sink2 = sink_ref[hk * Q_PER_KV + g] * LOG2E
            m = jnp.maximum(jnp.max(sg, axis=-1, keepdims=True), sink2)
            p = jnp.exp2(sg - m)
            dens.append(jnp.sum(p, axis=-1, keepdims=True) + jnp.exp2(sink2 - m))
            ps.append(p.astype(BF16))
        pv = _dot(jnp.concatenate(ps, axis=0), vwin)
        for g in range(Q_PER_KV):
            h = hk * Q_PER_KV + g
            o = pv[g * BLOCK:(g + 1) * BLOCK] / dens[g]
            o_ref[0, :, h * HEAD_DIM:(h + 1) * HEAD_DIM] = o.astype(o_ref.dtype)


def windowed_attention(proj3, sink, cc, ss):
    nseq, l, _ = proj3.shape
    nb = l // BLOCK
    kcol, vcol = OFF_K // KV_W, OFF_V // KV_W
    prev = lambda n: jnp.maximum(n - 1, 0)
    nxt = lambda n: jnp.minimum(n + 1, nb - 1)
    kv = lambda rowf, col: pl.BlockSpec((1, BLOCK, KV_W), lambda b, n: (b, rowf(n), col))
    ident = lambda n: n
    return pl.pallas_call(
        _attn_kernel,
        grid=(nseq, nb),
        in_specs=[
            pl.BlockSpec(memory_space=pltpu.SMEM),
            pl.BlockSpec((1, BLOCK, ATTN_W), lambda b, n: (b, n, OFF_Q // ATTN_W)),
            kv(prev, kcol), kv(ident, kcol), kv(nxt, kcol),
            kv(prev, vcol), kv(ident, vcol), kv(nxt, vcol),
            pl.BlockSpec((l, HEAD_DIM), lambda b, n: (0, 0)),
            pl.BlockSpec((l, HEAD_DIM), lambda b, n: (0, 0)),
        ],
        out_specs=pl.BlockSpec((1, BLOCK, ATTN_W), lambda b, n: (b, n, 0)),
        out_shape=jax.ShapeDtypeStruct((nseq, l, ATTN_W), BF16),
        compiler_params=_params("parallel", "arbitrary"),
        name="windowed_attention",
    )(sink, proj3, proj3, proj3, proj3, proj3, proj3, proj3, cc, ss)


HALO = SUBLANES
PAD = D_CONV // 2
CONV_PHASES = 4
CONV_SUB = 64
CONV_BLK = CONV_PHASES * CONV_SUB


def _conv_kernel(x_ref, w_ref, b_ref, o_ref, edge_ref, *res_ref):
    rows = x_ref.shape[1]
    nblk = rows // CONV_BLK
    w = [w_ref[j:j + 1, :] for j in range(D_CONV)]
    bias = b_ref[...]

    def emit(tap, out_base):
        for r in range(CONV_PHASES):
            acc = bias
            for j in range(D_CONV):
                acc = acc + tap(r + j - PAD) * w[j]
            y = _silu(acc)
            if res_ref:
                res_ref[0][pl.ds(out_base + r, CONV_SUB, stride=CONV_PHASES), :] = y
            else:
                o_ref[0, pl.ds(out_base + r, CONV_SUB, stride=CONV_PHASES), :] = y

    def body(i, c):
        base = i * CONV_BLK
        emit(lambda off: x_ref[0, pl.ds(base + off, CONV_SUB, stride=CONV_PHASES), :], base)
        return c

    lax.fori_loop(1, nblk - 1, body, 0)

    zeros = jnp.zeros((HALO, LANES), F32)
    edge_tap = lambda slot: (lambda off: edge_ref[slot, pl.ds(HALO + off, CONV_SUB, stride=CONV_PHASES), :])
    edge_ref[0, 0:HALO, :] = zeros
    if nblk == 1:
        edge_ref[0, HALO:HALO + rows, :] = x_ref[0]
        edge_ref[0, HALO + rows:, :] = zeros
        emit(edge_tap(0), 0)
    else:
        edge_ref[0, HALO:, :] = x_ref[0, 0:CONV_BLK + HALO, :]
        edge_ref[1, 0:HALO + CONV_BLK, :] = x_ref[0, rows - CONV_BLK - HALO:rows, :]
        edge_ref[1, HALO + CONV_BLK:, :] = zeros
        emit(edge_tap(0), 0)
        emit(edge_tap(1), rows - CONV_BLK)
    if res_ref:
        o_ref[0] = res_ref[0][...].astype(o_ref.dtype)


def conv_silu(proj3, conv_w, conv_b, *, ch_off, n_ch, out_dtype):
    nseq, l, _ = proj3.shape
    col0 = (OFF_XBC + ch_off) // LANES
    wcol0 = ch_off // LANES
    scratch = [pltpu.VMEM((2, CONV_BLK + 2 * HALO, LANES), F32)]
    if out_dtype != F32:
        scratch.append(pltpu.VMEM((l, LANES), F32))
    return pl.pallas_call(
        _conv_kernel,
        grid=(nseq, n_ch // LANES),
        in_specs=[
            pl.BlockSpec((1, l, LANES), lambda b, c: (b, 0, col0 + c)),
            pl.BlockSpec((D_CONV, LANES), lambda b, c: (0, wcol0 + c)),
            pl.BlockSpec((1, LANES), lambda b, c: (0, wcol0 + c)),
        ],
        out_specs=pl.BlockSpec((1, l, LANES), lambda b, c: (b, 0, c)),
        out_shape=jax.ShapeDtypeStruct((nseq, l, n_ch), out_dtype),
        scratch_shapes=scratch,
        compiler_params=_params("parallel", "parallel"),
        name="conv_silu",
    )(proj3, conv_w, conv_b)


DT_CHUNKS = 8
PAIRS = HEADS_PER_GROUP // 2
DH = 2 * SSD_HEADS
COL_AF, COL_AB, COL_GF, COL_GB = (k * HEADS_PER_GROUP for k in range(4))
COL_USED = 4 * HEADS_PER_GROUP
GROUP_LANES = SSD_HEADS // 2


def _split3(a):
    hi = a.astype(BF16)
    r = a - hi.astype(F32)
    mid = r.astype(BF16)
    lo = (r - mid.astype(F32)).astype(BF16)
    return hi, mid, lo


def _tri_sum(tri, a):
    hi, mid, lo = _split3(a)
    return _dot(tri, hi) + _dot(tri, mid) + _dot(tri, lo)


def _pair_rows(v, row_lo, lane_even):
    swap = lambda u: jnp.concatenate([u[CHUNK // 2:], u[:CHUNK // 2]], axis=0)
    from_prev_lane = swap(pltpu.roll(v, 1, 1))
    from_next_lane = swap(pltpu.roll(v, DH - 1, 1))
    w = jnp.where(row_lo, jnp.where(lane_even, v, from_prev_lane), jnp.where(lane_even, from_next_lane, v))
    return w.T


def _dt_kernel(raw_ref, bias_ref, alog_ref, col_ref, rowg_ref, rowdt_ref, tot_ref):
    neg_a2 = -jnp.exp(alog_ref[...]) * LOG2E
    bias = bias_ref[...]
    r = lax.broadcasted_iota(jnp.int32, (CHUNK, CHUNK), 0)
    c = lax.broadcasted_iota(jnp.int32, (CHUNK, CHUNK), 1)
    lower = (r >= c).astype(BF16)
    upper = (r <= c).astype(BF16)
    fwd_lane = c < SSD_HEADS
    row_lo = r < CHUNK // 2
    lane_even = (c & 1) == 0
    src0 = ((c >> 3) & 1) * SSD_HEADS + (c & (HEADS_PER_GROUP - 1))
    low_groups = (c & GROUP_LANES) == 0
    lane8 = lax.broadcasted_iota(jnp.int32, (SUBLANES, DH), 1)
    for ci in range(raw_ref.shape[0] // CHUNK):
        rows = slice(ci * CHUNK, (ci + 1) * CHUNK)
        raw = raw_ref[rows, :] + bias
        dt = jnp.maximum(raw, 0.0) + jnp.log1p(jnp.exp(-jnp.abs(raw)))
        a2 = dt * neg_a2
        pre = _tri_sum(lower, a2)
        suf = _tri_sum(upper, a2)
        acum2 = jnp.where(fwd_lane, pre, suf)
        g = jnp.log2(dt) - acum2
        tot8 = jnp.broadcast_to(jnp.where(fwd_lane[0:1], pre[CHUNK - 1:CHUNK], suf[0:1]), (SUBLANES, DH))
        for d in range(2):
            for m in range(SSD_HEADS // 2):
                idx = jnp.where(lane8 < SSD_HEAD_DIM, d * SSD_HEADS + 2 * m, d * SSD_HEADS + 2 * m + 1)
                spread = jnp.take_along_axis(tot8, idx, axis=1)
                tot_ref[ci, d, :, m * LANES:(m + 1) * LANES] = spread[0:1]
        rowg_ref[ci] = _pair_rows(g, row_lo, lane_even)
        rowdt_ref[ci] = _pair_rows(dt, row_lo, lane_even)
        sources = (jnp.where(low_groups, acum2, pltpu.roll(g, GROUP_LANES, 1)),
                   jnp.where(low_groups, pltpu.roll(g, DH - GROUP_LANES, 1), acum2))
        for grp in range(SSD_GROUPS):
            hi = grp >= SSD_GROUPS // 2
            idx = src0 + grp * HEADS_PER_GROUP + jnp.where(c < COL_GF, 0, -GROUP_LANES if hi else GROUP_LANES)
            packed = jnp.take_along_axis(sources[hi], idx, axis=1)
            col_ref[grp, rows, :] = jnp.where(c < COL_USED, packed, 0.0)


def dt_prepare(proj, dt_bias, a_log):
    t = proj.shape[0]
    rows = DT_CHUNKS * CHUNK
    if t % rows:
        rows = CHUNK
    cps = rows // CHUNK
    nchunks = t // CHUNK
    col = OFF_DT // DH
    return pl.pallas_call(
        _dt_kernel,
        grid=(t // rows,),
        in_specs=[
            pl.BlockSpec((rows, DH), lambda i: (i, col)),
            pl.BlockSpec((1, DH), lambda i: (0, 0)),
            pl.BlockSpec((1, DH), lambda i: (0, 0)),
        ],
        out_specs=[
            pl.BlockSpec((SSD_GROUPS, rows, LANES), lambda i: (0, i, 0)),
            pl.BlockSpec((cps, DH, CHUNK), lambda i: (i, 0, 0)),
            pl.BlockSpec((cps, DH, CHUNK), lambda i: (i, 0, 0)),
            pl.BlockSpec((cps, 2, 1, D_INNER), lambda i: (i, 0, 0, 0)),
        ],
        out_shape=[
            jax.ShapeDtypeStruct((SSD_GROUPS, t, LANES), F32),
            jax.ShapeDtypeStruct((nchunks, DH, CHUNK), F32),
            jax.ShapeDtypeStruct((nchunks, DH, CHUNK), F32),
            jax.ShapeDtypeStruct((nchunks, 2, 1, D_INNER), F32),
        ],
        compiler_params=_params("parallel"),
        name="dt_prepare",
    )(proj, dt_bias, a_log)


SSD_CPS = 16
HALF = CHUNK // 2


def _expand(colv, base, lane_lo):
    tiles = []
    for p in range(PAIRS):
        idx = jnp.where(lane_lo, base + 2 * p, base + 2 * p + 1)
        tiles.append(jnp.take_along_axis(colv, idx, axis=1))
    return jnp.concatenate(tiles, axis=1)


def _ssd_kernel(x_ref, b_ref, c_ref, z_ref, col_ref, rowgf_ref, rowgb_ref, rowdtf_ref,
                tot_ref, skip_ref, ng_ref, o_ref, sf_ref, sb_ref, sb_all_ref):
    ph = pl.program_id(2)
    cs = pl.program_id(3)
    ncs = pl.num_programs(3)
    cps = x_ref.shape[1] // CHUNK
    lane = lax.broadcasted_iota(jnp.int32, (CHUNK, LANES), 1)
    lane_lo = lane < SSD_HEAD_DIM

    def state_update(s_ref, xs, bc, g_x, tot_row):
        xw = (xs * jnp.exp2(tot_row + g_x)).astype(BF16)
        st = lax.dot_general(bc, xw, (((0,), (0,)), ((), ())), preferred_element_type=F32)
        s_ref[...] = s_ref[...] * jnp.exp2(tot_row) + st

    @pl.when(ph == 0)
    def _backward_states():
        @pl.when(cs == 0)
        def _():
            sb_ref[...] = jnp.zeros_like(sb_ref)

        for k in range(cps):
            ci = cps - 1 - k
            rows = slice(ci * CHUNK, (ci + 1) * CHUNK)
            chunk = (ncs - 1 - cs) * cps + ci
            sb_all_ref[chunk] = sb_ref[...].astype(BF16)
            gb = _expand(col_ref[0, 0, rows, :], COL_GB, lane_lo)
            state_update(sb_ref, x_ref[0, rows, :], b_ref[0, rows, :], gb, tot_ref[0, ci, 1])

    @pl.when(ph == 1)
    def _forward_and_output():
        @pl.when(cs == 0)
        def _():
            sf_ref[...] = jnp.zeros_like(sf_ref)

        li = lax.broadcasted_iota(jnp.int32, (HALF, LANES), 0)
        s_in_half = lax.broadcasted_iota(jnp.int32, (HALF, LANES), 1) & (HALF - 1)
        fwd = li > s_in_half
        diag = li == s_in_half
        for ci in range(cps):
            rows = slice(ci * CHUNK, (ci + 1) * CHUNK)
            chunk = cs * cps + ci
            xs = x_ref[0, rows, :]
            xs16 = xs.astype(BF16)
            bc = b_ref[0, rows, :]
            cc = c_ref[0, rows, :]
            cb = lax.dot_general(cc, bc, (((1,), (1,)), ((), ())), preferred_element_type=F32)
            cb_rot = pltpu.roll(cb, HALF, 1)
            cbs = [jnp.where(lane_lo, cb, cb_rot), jnp.where(lane_lo, cb_rot, cb)]
            colv = col_ref[0, 0, rows, :]
            af = _expand(colv, COL_AF, lane_lo)
            ab = _expand(colv, COL_AB, lane_lo)
            gf = _expand(colv, COL_GF, lane_lo)

            ydiag = []
            for p in range(PAIRS):
                afp = af[:, p * LANES:(p + 1) * LANES]
                abp = ab[:, p * LANES:(p + 1) * LANES]
                tiles = []
                for half in (0, 1):
                    rr = slice(2 * p + half, 2 * p + half + 1)
                    rgf, rgb = rowgf_ref[0, ci, rr, :], rowgb_ref[0, ci, rr, :]
                    same = slice(half * HALF, (half + 1) * HALF)
                    e_same = jnp.exp2(jnp.where(fwd, afp[same] + rgf, abp[same] + rgb))
                    e_same = e_same + jnp.where(diag, rowdtf_ref[0, ci, rr, :], 0.0)
                    if half == 0:
                        e = jnp.concatenate([e_same, jnp.exp2(afp[HALF:] + rgf)], axis=0)
                    else:
                        e = jnp.concatenate([jnp.exp2(abp[:HALF] + rgb), e_same], axis=0)
                    tiles.append((cbs[half] * e).astype(BF16))
                xp = xs16[:, p * LANES:(p + 1) * LANES]
                zero = jnp.zeros_like(xp)
                x_lo = jnp.where(lane_lo, xp, zero)
                x_hi = jnp.where(lane_lo, zero, xp)
                rhs = jnp.concatenate([x_lo[:HALF], x_hi[:HALF], x_lo[HALF:], x_hi[HALF:]], axis=0)
                ydiag.append(_dot(jnp.concatenate(tiles, axis=1), rhs))
            y = jnp.concatenate(ydiag, axis=1)

            y = y + _dot(cc, sf_ref[...].astype(BF16)) * jnp.exp2(af)
            y = y + _dot(cc, sb_all_ref[chunk]) * jnp.exp2(ab)
            y = y + xs * skip_ref[...]

            state_update(sf_ref, xs, bc, gf, tot_ref[0, ci, 0])

            z = z_ref[0, rows, :]
            y = y * _silu(z)
            ms = jnp.mean(y * y, axis=-1, keepdims=True)
            o_ref[0, rows, :] = ((y * lax.rsqrt(ms + EPS)) * ng_ref[...]).astype(o_ref.dtype)


def ssd_scan(xs, bc, proj3, colpack, rowg, rowdt, tot_x, skip_x, norm_g):
    nseq, l, _ = xs.shape
    nchunks = l // CHUNK
    cps = SSD_CPS if nchunks % SSD_CPS == 0 else 1
    ncs = nchunks // cps
    rows = cps * CHUNK
    blk = lambda ph, cs: jnp.where(ph == 0, ncs - 1 - cs, cs)
    fwd_only = lambda ph, cs: jnp.where(ph == 0, 0, cs)
    zcol = OFF_Z // GROUP_W
    row_spec = lambda d: pl.BlockSpec(
        (1, cps, HEADS_PER_GROUP, CHUNK), lambda b, g, ph, cs: (b, blk(ph, cs), d * SSD_GROUPS + g, 0))
    return pl.pallas_call(
        _ssd_kernel,
        grid=(nseq, SSD_GROUPS, 2, ncs),
        in_specs=[
            pl.BlockSpec((1, rows, GROUP_W), lambda b, g, ph, cs: (b, blk(ph, cs), g)),
            pl.BlockSpec((1, rows, D_STATE), lambda b, g, ph, cs: (b, blk(ph, cs), g)),
            pl.BlockSpec((1, rows, D_STATE), lambda b, g, ph, cs: (b, fwd_only(ph, cs), SSD_GROUPS + g)),
            pl.BlockSpec((1, rows, GROUP_W), lambda b, g, ph, cs: (b, fwd_only(ph, cs), zcol + g)),
            pl.BlockSpec((1, 1, rows, LANES), lambda b, g, ph, cs: (g, b, blk(ph, cs), 0)),
            row_spec(0), row_spec(1), row_spec(0),
            pl.BlockSpec((1, cps, 2, 1, GROUP_W), lambda b, g, ph, cs: (b, blk(ph, cs), 0, 0, g)),
            pl.BlockSpec((1, GROUP_W), lambda b, g, ph, cs: (0, g)),
            pl.BlockSpec((1, GROUP_W), lambda b, g, ph, cs: (0, g)),
        ],
        out_specs=pl.BlockSpec((1, rows, GROUP_W), lambda b, g, ph, cs: (b, fwd_only(ph, cs), g)),
        out_shape=jax.ShapeDtypeStruct((nseq, l, D_INNER), BF16),
        scratch_shapes=[
            pltpu.VMEM((D_STATE, GROUP_W), F32),
            pltpu.VMEM((D_STATE, GROUP_W), F32),
            pltpu.VMEM((nchunks, D_STATE, GROUP_W), BF16),
        ],
        compiler_params=_params("parallel", "parallel", "arbitrary", "arbitrary"),
        name="ssd_scan",
    )(xs, bc, bc, proj3, colpack, rowg, rowg, rowdt, tot_x, skip_x, norm_g)


def _rope_tables(l):
    half = HEAD_DIM // 2
    inv_freq = ROPE_THETA ** (-jnp.arange(half, dtype=F32) * 2.0 / HEAD_DIM)
    ang = jnp.arange(l, dtype=F32)[:, None] * inv_freq[None, :]
    cos, sin = jnp.cos(ang), jnp.sin(ang)
    return jnp.concatenate([cos, cos], axis=-1), jnp.concatenate([-sin, sin], axis=-1)


def _row_tile(t, want):
    while t % want:
        want //= 2
    return want


def _layer(layer, x, nseq, l, cc, ss, w_in, conv_w, conv_b, a_log, dt_bias, skip_x, ssd_norm, attn_sink,
           w_out_attn, w_out_ssd, w_out, norm_mix, norm_ffn, w_gate_up, w_down):
    t = nseq * l
    nchunks = l // CHUNK
    proj = norm_matmul(x, norm_mix, w_in, layer, tm=_row_tile(t, 1024), tn=IN_TILE, out_dtype=F32)
    proj3 = proj.reshape(nseq, l, N_IN)

    attn = windowed_attention(proj3, attn_sink, cc, ss).reshape(t, ATTN_W)

    xs = conv_silu(proj3, conv_w, conv_b, ch_off=0, n_ch=D_INNER, out_dtype=F32)
    bcm = conv_silu(proj3, conv_w, conv_b, ch_off=D_INNER, n_ch=CONV_CH - D_INNER, out_dtype=BF16)

    colpack, rowg, rowdt, tot = dt_prepare(proj, dt_bias, a_log)
    tot_x = tot.reshape(nseq, nchunks, 2, 1, D_INNER)
    colpack = colpack.reshape(SSD_GROUPS, nseq, l, LANES)
    rowg = rowg.reshape(nseq, nchunks, DH, CHUNK)
    rowdt = rowdt.reshape(nseq, nchunks, DH, CHUNK)
    ssd = ssd_scan(xs, bcm, proj3, colpack, rowg, rowdt, tot_x, skip_x, ssd_norm).reshape(t, D_INNER)

    mix = gated_mix(attn, ssd, w_out_attn, w_out_ssd, layer, proj, tm=_row_tile(t, 1024), tn=512)
    x = resid_matmul(mix, w_out, layer, x, tm=_row_tile(t, 1024), tn=1024)
    act = norm_swiglu(x, norm_ffn, w_gate_up, layer, tm=_row_tile(t, 1024), tn=512)
    x = resid_matmul(act, w_down, layer, x, tm=_row_tile(t, 1024), tn=512)
    return x


def _trunk(x, w_in, conv_w, conv_b, a_log, dt_bias, d_skip, ssd_norm, attn_sink,
           w_out_attn, w_out_ssd, w_out, norm_mix, norm_ffn, w_gate_up, w_down):
    nseq, l, _ = x.shape
    x = x.reshape(nseq * l, D_MODEL)
    cc, ss = _rope_tables(l)
    w_in16 = w_in.astype(BF16)
    mats = [w.astype(BF16) for w in (w_out_attn, w_out_ssd, w_out)]
    ffn = [w.astype(BF16) for w in (w_gate_up, w_down)]
    for i in range(DEPTH):
        x = _layer(
            i, x, nseq, l, cc, ss, w_in16, conv_w[i], conv_b[i][None, :],
            a_log[i].reshape(1, -1), dt_bias[i].reshape(1, -1), jnp.repeat(d_skip[i], SSD_HEAD_DIM)[None, :],
            ssd_norm[i][None, :], attn_sink[i], *mats, norm_mix[i][None, :], norm_ffn[i][None, :], *ffn)
    return x


def kernel(x_prompt, x_sample, w_in, conv_w, conv_b, a_log, dt_bias, d_skip, ssd_norm, attn_sink,
           w_out_attn, w_out_ssd, w_out, norm_mix, norm_ffn, w_gate_up, w_down, final_norm):
    weights = (w_in, conv_w, conv_b, a_log, dt_bias, d_skip, ssd_norm, attn_sink,
               w_out_attn, w_out_ssd, w_out, norm_mix, norm_ffn, w_gate_up, w_down)
    g = final_norm[None, :]
    outs = []
    if x_prompt.shape[1] == x_sample.shape[1]:
        l = x_prompt.shape[1]
        x = _trunk(jnp.concatenate([x_prompt, x_sample], axis=0), *weights)
        row0 = 0
        for xin in (x_prompt, x_sample):
            n = xin.shape[0] * l
            outs.append(rmsnorm(x, g, tm=_row_tile(l, 512), row0=row0, nrows=n).reshape(xin.shape))
            row0 += n
    else:
        for xin in (x_prompt, x_sample):
            x = _trunk(xin, *weights)
            outs.append(rmsnorm(x, g, tm=_row_tile(x.shape[0], 512), row0=0, nrows=x.shape[0]).reshape(xin.shape))
    return tuple(outs)
```

```python
import functools
import math

import jax
import jax.numpy as jnp
from jax import lax
from jax.experimental import pallas as pl
from jax.experimental.pallas import tpu as pltpu

D_MODEL = 2048
DEPTH = 4
N_HEADS = 16
N_KV_HEADS = 4
HEAD_DIM = 128
Q_PER_KV = N_HEADS // N_KV_HEADS
ATTN_W = N_HEADS * HEAD_DIM
KV_W = N_KV_HEADS * HEAD_DIM
BLOCK = 128
ROPE_THETA = 10000.0
D_INNER = 2 * D_MODEL
SSD_HEAD_DIM = 64
SSD_HEADS = D_INNER // SSD_HEAD_DIM
SSD_GROUPS = 8
HEADS_PER_GROUP = SSD_HEADS // SSD_GROUPS
GROUP_W = D_INNER // SSD_GROUPS
D_STATE = 128
D_CONV = 5
CHUNK = 128
CONV_CH = D_INNER + 2 * SSD_GROUPS * D_STATE
D_FF = ((8 * D_MODEL // 3 + 255) // 256) * 256
EPS = 1e-6

OFF_Q = 0
OFF_K = OFF_Q + ATTN_W
OFF_V = OFF_K + KV_W
OFF_GATT = OFF_V + KV_W
OFF_GSSD = OFF_GATT + D_MODEL
OFF_Z = OFF_GSSD + D_MODEL
OFF_XBC = OFF_Z + D_INNER
OFF_DT = OFF_XBC + CONV_CH
N_IN = OFF_DT + 2 * SSD_HEADS
IN_TILE = 768

LANES = 128
SUBLANES = 8
VMEM_LIMIT = 56 * 1024 * 1024
NEG_BIG = -1e30
LOG2E = 1.4426950408889634

F32 = jnp.float32
BF16 = jnp.bfloat16


def _params(*sem):
    return pltpu.CompilerParams(dimension_semantics=sem, vmem_limit_bytes=VMEM_LIMIT)


def _sigmoid(x):
    return 1.0 / (1.0 + jnp.exp(-x))


def _silu(x):
    h = 0.5 * x
    return h + h * jnp.tanh(h)


def _dot(a, b):
    return jnp.dot(a, b, preferred_element_type=F32)


NORM_ROWS = 128


def _part_starts(parts, tm):
    starts, lo = [], 0
    for p in parts:
        assert p.shape[0] % tm == 0, "each part must hold whole row tiles"
        starts.append(lo)
        lo += p.shape[0] // tm
    return starts, lo


def _part_specs(parts, starts, block, col):
    return [pl.BlockSpec(block, lambda i, j, lo=lo, n=p.shape[0] // block[0]: (jnp.clip(i - lo, 0, n - 1), col(j)))
            for p, lo in zip(parts, starts)]


def _pick_part(starts, loads):
    i = pl.program_id(0)
    x = loads[0]()
    for lo, load in zip(starts[1:], loads[1:]):
        x = jnp.where(i >= lo, load(), x)
    return x


def _norm_into(x_refs, starts, g_ref, h_ref):
    g = g_ref[...]

    def body(i, c):
        r = pl.ds(pl.multiple_of(i * NORM_ROWS, NORM_ROWS), NORM_ROWS)
        x = _pick_part(starts, [lambda ref=ref: ref[r, :] for ref in x_refs])
        ms = jnp.mean(x * x, axis=-1, keepdims=True)
        h_ref[r, :] = ((x * lax.rsqrt(ms + EPS)) * g).astype(BF16)
        return c

    lax.fori_loop(0, h_ref.shape[0] // NORM_ROWS, body, 0)


def _norm_matmul_kernel(starts, *refs):
    x_refs, (g_ref, w_ref, o_ref, h_ref) = refs[:len(starts)], refs[len(starts):]

    @pl.when(pl.program_id(1) == 0)
    def _():
        _norm_into(x_refs, starts, g_ref, h_ref)

    o_ref[...] = _dot(h_ref[...], w_ref[...]).astype(o_ref.dtype)


def norm_matmul(parts, g, w, layer, *, tm, tn, out_dtype):
    k = parts[0].shape[1]
    n = w.shape[2]
    starts, tiles = _part_starts(parts, tm)
    return pl.pallas_call(
        functools.partial(_norm_matmul_kernel, starts),
        grid=(tiles, pl.cdiv(n, tn)),
        in_specs=_part_specs(parts, starts, (tm, k), lambda j: 0) + [
            pl.BlockSpec((1, k), lambda i, j: (0, 0)),
            pl.BlockSpec((None, k, tn), lambda i, j: (layer, 0, j)),
        ],
        out_specs=pl.BlockSpec((tm, tn), lambda i, j: (i, j)),
        out_shape=jax.ShapeDtypeStruct((tiles * tm, n), out_dtype),
        scratch_shapes=[pltpu.VMEM((tm, k), BF16)],
        compiler_params=_params("parallel", "arbitrary"),
        name="norm_in_proj",
    )(*parts, g, w)


def _norm_swiglu_kernel(x_ref, g_ref, wg_ref, wu_ref, o_ref, h_ref):
    @pl.when(pl.program_id(1) == 0)
    def _():
        _norm_into([x_ref], [0], g_ref, h_ref)

    h = h_ref[...]
    gt = _dot(h, wg_ref[...])
    up = _dot(h, wu_ref[...])
    o_ref[...] = (_silu(gt) * up).astype(o_ref.dtype)


def norm_swiglu(x, g, w_gate_up, layer, *, tm, tn):
    t, k = x.shape
    f = w_gate_up.shape[2] // 2
    nf = f // tn
    return pl.pallas_call(
        _norm_swiglu_kernel,
        grid=(t // tm, nf),
        in_specs=[
            pl.BlockSpec((tm, k), lambda i, j: (i, 0)),
            pl.BlockSpec((1, k), lambda i, j: (0, 0)),
            pl.BlockSpec((None, k, tn), lambda i, j: (layer, 0, j)),
            pl.BlockSpec((None, k, tn), lambda i, j: (layer, 0, j + nf)),
        ],
        out_specs=pl.BlockSpec((tm, tn), lambda i, j: (i, j)),
        out_shape=jax.ShapeDtypeStruct((t, f), BF16),
        scratch_shapes=[pltpu.VMEM((tm, k), BF16)],
        compiler_params=_params("parallel", "arbitrary"),
        name="norm_swiglu",
    )(x, g, w_gate_up, w_gate_up)


def _resid_matmul_kernel(starts, a_ref, w_ref, *refs):
    x_refs, o_ref = refs[:-1], refs[-1]
    x = _pick_part(starts, [lambda ref=ref: ref[...] for ref in x_refs])
    o_ref[...] = x + _dot(a_ref[...], w_ref[...])


def resid_matmul(a, w, layer, x_parts, *, tm, tn):
    t, k = a.shape
    n = w.shape[2]
    starts, tiles = _part_starts(x_parts, tm)
    assert tiles * tm == t
    return pl.pallas_call(
        functools.partial(_resid_matmul_kernel, starts),
        grid=(tiles, n // tn),
        in_specs=[
            pl.BlockSpec((tm, k), lambda i, j: (i, 0)),
            pl.BlockSpec((None, k, tn), lambda i, j: (layer, 0, j)),
        ] + _part_specs(x_parts, starts, (tm, tn), lambda j: j),
        out_specs=pl.BlockSpec((tm, tn), lambda i, j: (i, j)),
        out_shape=jax.ShapeDtypeStruct((t, n), F32),
        compiler_params=_params("parallel", "arbitrary"),
        name="resid_matmul",
    )(a, w, *x_parts)


def _mix_kernel(attn_ref, ssd_ref, wa_ref, ws_ref, ga_ref, gs_ref, o_ref):
    a = _dot(attn_ref[...], wa_ref[...])
    s = _dot(ssd_ref[...], ws_ref[...])
    o_ref[...] = (_sigmoid(ga_ref[...]) * a + _sigmoid(gs_ref[...]) * s).astype(o_ref.dtype)


def gated_mix(attn, ssd, wa, ws, layer, proj, *, tm, tn):
    t = attn.shape[0]
    n = wa.shape[2]
    return pl.pallas_call(
        _mix_kernel,
        grid=(t // tm, n // tn),
        in_specs=[
            pl.BlockSpec((tm, attn.shape[1]), lambda i, j: (i, 0)),
            pl.BlockSpec((tm, ssd.shape[1]), lambda i, j: (i, 0)),
            pl.BlockSpec((None, wa.shape[1], tn), lambda i, j: (layer, 0, j)),
            pl.BlockSpec((None, ws.shape[1], tn), lambda i, j: (layer, 0, j)),
            pl.BlockSpec((tm, tn), lambda i, j: (i, j + OFF_GATT // tn)),
            pl.BlockSpec((tm, tn), lambda i, j: (i, j + OFF_GSSD // tn)),
        ],
        out_specs=pl.BlockSpec((tm, tn), lambda i, j: (i, j)),
        out_shape=jax.ShapeDtypeStruct((t, n), BF16),
        compiler_params=_params("parallel", "arbitrary"),
        name="gated_mix",
    )(attn, ssd, wa, ws, proj, proj)


def _rmsnorm_kernel(x_ref, g_ref, o_ref):
    x = x_ref[...]
    ms = jnp.mean(x * x, axis=-1, keepdims=True)
    o_ref[...] = (x * lax.rsqrt(ms + EPS)) * g_ref[...]


def rmsnorm(x, g, *, tm, row0, nrows):
    k = x.shape[1]
    off = row0 // tm
    return pl.pallas_call(
        _rmsnorm_kernel,
        grid=(nrows // tm,),
        in_specs=[pl.BlockSpec((tm, k), lambda i: (i + off, 0)), pl.BlockSpec((1, k), lambda i: (0, 0))],
        out_specs=pl.BlockSpec((tm, k), lambda i: (i, 0)),
        out_shape=jax.ShapeDtypeStruct((nrows, k), F32),
        compiler_params=_params("parallel"),
        name="final_rmsnorm",
    )(x, g)


def _rope(x, cc, ss):
    return x * cc + pltpu.roll(x, HEAD_DIM // 2, 1) * ss


def _attn_kernel(sink_ref, q_ref, kp_ref, kc_ref, kn_ref, vp_ref, vc_ref, vn_ref, cc_ref, ss_ref, o_ref):
    n = pl.program_id(1)
    nb = pl.num_programs(1)
    pos_c = pl.multiple_of(n * BLOCK, BLOCK)
    pos_p = pl.multiple_of(jnp.maximum(n - 1, 0) * BLOCK, BLOCK)
    pos_n = pl.multiple_of(jnp.minimum(n + 1, nb - 1) * BLOCK, BLOCK)
    cc_c, ss_c = cc_ref[pl.ds(pos_c, BLOCK), :], ss_ref[pl.ds(pos_c, BLOCK), :]
    cc_k = jnp.concatenate([cc_ref[pl.ds(pos_p, BLOCK), :], cc_c, cc_ref[pl.ds(pos_n, BLOCK), :]], axis=0)
    ss_k = jnp.concatenate([ss_ref[pl.ds(pos_p, BLOCK), :], ss_c, ss_ref[pl.ds(pos_n, BLOCK), :]], axis=0)
    cc_q = jnp.concatenate([cc_c] * Q_PER_KV, axis=0)
    ss_q = jnp.concatenate([ss_c] * Q_PER_KV, axis=0)

    rows = Q_PER_KV * BLOCK
    t = lax.broadcasted_iota(jnp.int32, (rows, BLOCK), 0) % BLOCK
    sk = lax.broadcasted_iota(jnp.int32, (rows, BLOCK), 1)
    valid_prev = sk >= t + jnp.where(n > 0, 0, BLOCK)
    valid_next = sk <= t - jnp.where(n < nb - 1, 0, BLOCK)
    scale2 = HEAD_DIM ** -0.5 * LOG2E

    for hk in range(N_KV_HEADS):
        c = slice(hk * HEAD_DIM, (hk + 1) * HEAD_DIM)
        kwin = jnp.concatenate([kp_ref[0, :, c], kc_ref[0, :, c], kn_ref[0, :, c]], axis=0)
        vwin = jnp.concatenate([vp_ref[0, :, c], vc_ref[0, :, c], vn_ref[0, :, c]], axis=0).astype(BF16)
        kr = _rope(kwin, cc_k, ss_k).astype(BF16)
        qg = jnp.concatenate(
            [q_ref[0, :, (hk * Q_PER_KV + g) * HEAD_DIM:(hk * Q_PER_KV + g + 1) * HEAD_DIM] for g in range(Q_PER_KV)],
            axis=0,
        )
        qr = _rope(qg, cc_q, ss_q).astype(BF16)
        s = lax.dot_general(qr, kr, (((1,), (1,)), ((), ())), preferred_element_type=F32) * scale2
        s = jnp.concatenate([jnp.where(valid_prev, s[:, :BLOCK], NEG_BIG), s[:, BLOCK:2 * BLOCK],
                             jnp.where(valid_next, s[:, 2 * BLOCK:], NEG_BIG)], axis=1)
        ps, dens = [], []
        for g in range(Q_PER_KV):
            sg = s[g * BLOCK:(g + 1) * BLOCK]
            sink2 = sink_ref[hk * Q_PER_KV + g] * LOG2E
            m = jnp.maximum(jnp.max(sg, axis=-1, keepdims=True), sink2)
            p = jnp.exp2(sg - m)
            dens.append(jnp.sum(p, axis=-1, keepdims=True) + jnp.exp2(sink2 - m))
            ps.append(p.astype(BF16))
        pv = _dot(jnp.concatenate(ps, axis=0), vwin)
        for g in range(Q_PER_KV):
            h = hk * Q_PER_KV + g
            o = pv[g * BLOCK:(g + 1) * BLOCK] / dens[g]
            o_ref[0, :, h * HEAD_DIM:(h + 1) * HEAD_DIM] = o.astype(o_ref.dtype)


def windowed_attention(proj3, sink, cc, ss):
    nseq, l, _ = proj3.shape
    nb = l // BLOCK
    kcol, vcol = OFF_K // KV_W, OFF_V // KV_W
    prev = lambda n: jnp.maximum(n - 1, 0)
    nxt = lambda n: jnp.minimum(n + 1, nb - 1)
    kv = lambda rowf, col: pl.BlockSpec((1, BLOCK, KV_W), lambda b, n: (b, rowf(n), col))
    ident = lambda n: n
    return pl.pallas_call(
        _attn_kernel,
        grid=(nseq, nb),
        in_specs=[
            pl.BlockSpec(memory_space=pltpu.SMEM),
            pl.BlockSpec((1, BLOCK, ATTN_W), lambda b, n: (b, n, OFF_Q // ATTN_W)),
            kv(prev, kcol), kv(ident, kcol), kv(nxt, kcol),
            kv(prev, vcol), kv(ident, vcol), kv(nxt, vcol),
            pl.BlockSpec((l, HEAD_DIM), lambda b, n: (0, 0)),
            pl.BlockSpec((l, HEAD_DIM), lambda b, n: (0, 0)),
        ],
        out_specs=pl.BlockSpec((1, BLOCK, ATTN_W), lambda b, n: (b, n, 0)),
        out_shape=jax.ShapeDtypeStruct((nseq, l, ATTN_W), BF16),
        compiler_params=_params("parallel", "arbitrary"),
        name="windowed_attention",
    )(sink, proj3, proj3, proj3, proj3, proj3, proj3, proj3, cc, ss)


HALO = SUBLANES
PAD = D_CONV // 2
CONV_PHASES = 4
CONV_SUB = 64
CONV_BLK = CONV_PHASES * CONV_SUB


def _conv_kernel(x_ref, w_ref, b_ref, o_ref, edge_ref, *res_ref):
    rows = x_ref.shape[1]
    nblk = rows // CONV_BLK
    w = [w_ref[j:j + 1, :] for j in range(D_CONV)]
    bias = b_ref[...]

    def emit(tap, out_base):
        for r in range(CONV_PHASES):
            acc = bias
            for j in range(D_CONV):
                acc = acc + tap(r + j - PAD) * w[j]
            y = _silu(acc)
            if res_ref:
                res_ref[0][pl.ds(out_base + r, CONV_SUB, stride=CONV_PHASES), :] = y
            else:
                o_ref[0, pl.ds(out_base + r, CONV_SUB, stride=CONV_PHASES), :] = y

    def body(i, c):
        base = i * CONV_BLK
        emit(lambda off: x_ref[0, pl.ds(base + off, CONV_SUB, stride=CONV_PHASES), :], base)
        return c

    lax.fori_loop(1, nblk - 1, body, 0)

    zeros = jnp.zeros((HALO, LANES), F32)
    edge_tap = lambda slot: (lambda off: edge_ref[slot, pl.ds(HALO + off, CONV_SUB, stride=CONV_PHASES), :])
    edge_ref[0, 0:HALO, :] = zeros
    if nblk == 1:
        edge_ref[0, HALO:HALO + rows, :] = x_ref[0]
        edge_ref[0, HALO + rows:, :] = zeros
        emit(edge_tap(0), 0)
    else:
        edge_ref[0, HALO:, :] = x_ref[0, 0:CONV_BLK + HALO, :]
        edge_ref[1, 0:HALO + CONV_BLK, :] = x_ref[0, rows - CONV_BLK - HALO:rows, :]
        edge_ref[1, HALO + CONV_BLK:, :] = zeros
        emit(edge_tap(0), 0)
        emit(edge_tap(1), rows - CONV_BLK)
    if res_ref:
        o_ref[0] = res_ref[0][...].astype(o_ref.dtype)


def conv_silu(proj3, conv_w, conv_b, *, ch_off, n_ch, out_dtype):
    nseq, l, _ = proj3.shape
    col0 = (OFF_XBC + ch_off) // LANES
    wcol0 = ch_off // LANES
    scratch = [pltpu.VMEM((2, CONV_BLK + 2 * HALO, LANES), F32)]
    if out_dtype != F32:
        scratch.append(pltpu.VMEM((l, LANES), F32))
    return pl.pallas_call(
        _conv_kernel,
        grid=(nseq, n_ch // LANES),
        in_specs=[
            pl.BlockSpec((1, l, LANES), lambda b, c: (b, 0, col0 + c)),
            pl.BlockSpec((D_CONV, LANES), lambda b, c: (0, wcol0 + c)),
            pl.BlockSpec((1, LANES), lambda b, c: (0, wcol0 + c)),
        ],
        out_specs=pl.BlockSpec((1, l, LANES), lambda b, c: (b, 0, c)),
        out_shape=jax.ShapeDtypeStruct((nseq, l, n_ch), out_dtype),
        scratch_shapes=scratch,
        compiler_params=_params("parallel", "parallel"),
        name="conv_silu",
    )(proj3, conv_w, conv_b)


DT_CHUNKS = 8
PAIRS = HEADS_PER_GROUP // 2
DH = 2 * SSD_HEADS
COL_AF, COL_AB, COL_GF, COL_GB = (k * HEADS_PER_GROUP for k in range(4))
COL_USED = 4 * HEADS_PER_GROUP
GROUP_LANES = SSD_HEADS // 2


def _split3(a):
    hi = a.astype(BF16)
    r = a - hi.astype(F32)
    mid = r.astype(BF16)
    lo = (r - mid.astype(F32)).astype(BF16)
    return hi, mid, lo


def _tri_sum(tri, a):
    hi, mid, lo = _split3(a)
    return _dot(tri, hi) + _dot(tri, mid) + _dot(tri, lo)


def _pair_rows(v, row_lo, lane_even):
    swap = lambda u: jnp.concatenate([u[CHUNK // 2:], u[:CHUNK // 2]], axis=0)
    from_prev_lane = swap(pltpu.roll(v, 1, 1))
    from_next_lane = swap(pltpu.roll(v, DH - 1, 1))
    w = jnp.where(row_lo, jnp.where(lane_even, v, from_prev_lane), jnp.where(lane_even, from_next_lane, v))
    return w.T


def _dt_kernel(raw_ref, bias_ref, alog_ref, col_ref, rowg_ref, rowdt_ref, tot_ref):
    neg_a2 = -jnp.exp(alog_ref[...]) * LOG2E
    bias = bias_ref[...]
    r = lax.broadcasted_iota(jnp.int32, (CHUNK, CHUNK), 0)
    c = lax.broadcasted_iota(jnp.int32, (CHUNK, CHUNK), 1)
    lower = (r >= c).astype(BF16)
    upper = (r <= c).astype(BF16)
    fwd_lane = c < SSD_HEADS
    row_lo = r < CHUNK // 2
    lane_even = (c & 1) == 0
    src0 = ((c >> 3) & 1) * SSD_HEADS + (c & (HEADS_PER_GROUP - 1))
    low_groups = (c & GROUP_LANES) == 0
    lane8 = lax.broadcasted_iota(jnp.int32, (DT_CHUNKS, DH), 1)
    totals = []
    for ci in range(DT_CHUNKS):
        rows = slice(ci * CHUNK, (ci + 1) * CHUNK)
        raw = raw_ref[rows, :] + bias
        dt = jnp.maximum(raw, 0.0) + jnp.log1p(jnp.exp(-jnp.abs(raw)))
        a2 = dt * neg_a2
        pre = _tri_sum(lower, a2)
        suf = _tri_sum(upper, a2)
        acum2 = jnp.where(fwd_lane, pre, suf)
        g = jnp.log2(dt) - acum2
        totals.append(jnp.where(fwd_lane[0:1], pre[CHUNK - 1:CHUNK], suf[0:1]))
        rowg_ref[ci] = _pair_rows(g, row_lo, lane_even)
        rowdt_ref[ci] = _pair_rows(dt, row_lo, lane_even)
        sources = (jnp.where(low_groups, acum2, pltpu.roll(g, GROUP_LANES, 1)),
                   jnp.where(low_groups, pltpu.roll(g, DH - GROUP_LANES, 1), acum2))
        for grp in range(SSD_GROUPS):
            hi = grp >= SSD_GROUPS // 2
            idx = src0 + grp * HEADS_PER_GROUP + jnp.where(c < COL_GF, 0, -GROUP_LANES if hi else GROUP_LANES)
            packed = jnp.take_along_axis(sources[hi], idx, axis=1)
            col_ref[grp, rows, :] = jnp.where(c < COL_USED, packed, 0.0)
    tot8 = jnp.concatenate(totals, axis=0)
    for d in range(2):
        for m in range(SSD_HEADS // 2):
            idx = jnp.where(lane8 < SSD_HEAD_DIM, d * SSD_HEADS + 2 * m, d * SSD_HEADS + 2 * m + 1)
            tot_ref[d, :, m * LANES:(m + 1) * LANES] = jnp.take_along_axis(tot8, idx, axis=1)


def dt_prepare(proj, dt_bias, a_log):
    t = proj.shape[0]
    rows = DT_CHUNKS * CHUNK
    assert t % rows == 0, "token count must be a multiple of DT_CHUNKS chunks"
    cps = DT_CHUNKS
    nchunks = t // CHUNK
    col = OFF_DT // DH
    return pl.pallas_call(
        _dt_kernel,
        grid=(t // rows,),
        in_specs=[
            pl.BlockSpec((rows, DH), lambda i: (i, col)),
            pl.BlockSpec((1, DH), lambda i: (0, 0)),
            pl.BlockSpec((1, DH), lambda i: (0, 0)),
        ],
        out_specs=[
            pl.BlockSpec((SSD_GROUPS, rows, LANES), lambda i: (0, i, 0)),
            pl.BlockSpec((cps, DH, CHUNK), lambda i: (i, 0, 0)),
            pl.BlockSpec((cps, DH, CHUNK), lambda i: (i, 0, 0)),
            pl.BlockSpec((2, cps, D_INNER), lambda i: (0, i, 0)),
        ],
        out_shape=[
            jax.ShapeDtypeStruct((SSD_GROUPS, t, LANES), F32),
            jax.ShapeDtypeStruct((nchunks, DH, CHUNK), F32),
            jax.ShapeDtypeStruct((nchunks, DH, CHUNK), F32),
            jax.ShapeDtypeStruct((2, nchunks, D_INNER), F32),
        ],
        compiler_params=_params("parallel"),
        name="dt_prepare",
    )(proj, dt_bias, a_log)


SSD_CPS = 16
HALF = CHUNK // 2


def _expand(colv, base, lane_lo):
    tiles = []
    for p in range(PAIRS):
        idx = jnp.where(lane_lo, base + 2 * p, base + 2 * p + 1)
        tiles.append(jnp.take_along_axis(colv, idx, axis=1))
    return jnp.concatenate(tiles, axis=1)


def _ssd_kernel(x_ref, b_ref, c_ref, z_ref, col_ref, rowgf_ref, rowgb_ref, rowdtf_ref,
                tot_ref, skip_ref, ng_ref, o_ref, sf_ref, sb_ref, sb_all_ref):
    ph = pl.program_id(2)
    cs = pl.program_id(3)
    ncs = pl.num_programs(3)
    cps = x_ref.shape[1] // CHUNK
    lane = lax.broadcasted_iota(jnp.int32, (CHUNK, LANES), 1)
    lane_lo = lane < SSD_HEAD_DIM

    def state_update(s_ref, xs, bc, g_x, tot_row):
        xw = (xs * jnp.exp2(tot_row + g_x)).astype(BF16)
        st = lax.dot_general(bc, xw, (((0,), (0,)), ((), ())), preferred_element_type=F32)
        s_ref[...] = s_ref[...] * jnp.exp2(tot_row) + st

    @pl.when(ph == 0)
    def _backward_states():
        @pl.when(cs == 0)
        def _():
            sb_ref[...] = jnp.zeros_like(sb_ref)

        for k in range(cps):
            ci = cps - 1 - k
            rows = slice(ci * CHUNK, (ci + 1) * CHUNK)
            chunk = (ncs - 1 - cs) * cps + ci
            sb_all_ref[chunk] = sb_ref[...].astype(BF16)
            gb = _expand(col_ref[0, 0, rows, :], COL_GB, lane_lo)
            state_update(sb_ref, x_ref[0, rows, :], b_ref[0, rows, :], gb, tot_ref[1, ci:ci + 1, :])

    @pl.when(ph == 1)
    def _forward_and_output():
        @pl.when(cs == 0)
        def _():
            sf_ref[...] = jnp.zeros_like(sf_ref)

        li = lax.broadcasted_iota(jnp.int32, (HALF, LANES), 0)
        s_in_half = lax.broadcasted_iota(jnp.int32, (HALF, LANES), 1) & (HALF - 1)
        fwd = li > s_in_half
        diag = li == s_in_half
        for ci in range(cps):
            rows = slice(ci * CHUNK, (ci + 1) * CHUNK)
            chunk = cs * cps + ci
            xs = x_ref[0, rows, :]
            xs16 = xs.astype(BF16)
            bc = b_ref[0, rows, :]
            cc = c_ref[0, rows, :]
            cb = lax.dot_general(cc, bc, (((1,), (1,)), ((), ())), preferred_element_type=F32)
            cb_rot = pltpu.roll(cb, HALF, 1)
            cbs = [jnp.where(lane_lo, cb, cb_rot), jnp.where(lane_lo, cb_rot, cb)]
            colv = col_ref[0, 0, rows, :]
            af = _expand(colv, COL_AF, lane_lo)
            ab = _expand(colv, COL_AB, lane_lo)
            gf = _expand(colv, COL_GF, lane_lo)

            ydiag = []
            for p in range(PAIRS):
                afp = af[:, p * LANES:(p + 1) * LANES]
                abp = ab[:, p * LANES:(p + 1) * LANES]
                tiles = []
                for half in (0, 1):
                    rr = slice(2 * p + half, 2 * p + half + 1)
                    rgf, rgb = rowgf_ref[0, ci, rr, :], rowgb_ref[0, ci, rr, :]
                    same = slice(half * HALF, (half + 1) * HALF)
                    e_same = jnp.exp2(jnp.where(fwd, afp[same] + rgf, abp[same] + rgb))
                    e_same = e_same + jnp.where(diag, rowdtf_ref[0, ci, rr, :], 0.0)
                    if half == 0:
                        e = jnp.concatenate([e_same, jnp.exp2(afp[HALF:] + rgf)], axis=0)
                    else:
                        e = jnp.concatenate([jnp.exp2(abp[:HALF] + rgb), e_same], axis=0)
                    tiles.append((cbs[half] * e).astype(BF16))
                xp = xs16[:, p * LANES:(p + 1) * LANES]
                zero = jnp.zeros_like(xp)
                x_lo = jnp.where(lane_lo, xp, zero)
                x_hi = jnp.where(lane_lo, zero, xp)
                rhs = jnp.concatenate([x_lo[:HALF], x_hi[:HALF], x_lo[HALF:], x_hi[HALF:]], axis=0)
                ydiag.append(_dot(jnp.concatenate(tiles, axis=1), rhs))
            y = jnp.concatenate(ydiag, axis=1)

            y = y + _dot(cc, sf_ref[...].astype(BF16)) * jnp.exp2(af)
            y = y + _dot(cc, sb_all_ref[chunk]) * jnp.exp2(ab)
            y = y + xs * skip_ref[...]

            state_update(sf_ref, xs, bc, gf, tot_ref[0, ci:ci + 1, :])

            z = z_ref[0, rows, :]
            y = y * _silu(z)
            ms = jnp.mean(y * y, axis=-1, keepdims=True)
            o_ref[0, rows, :] = ((y * lax.rsqrt(ms + EPS)) * ng_ref[...]).astype(o_ref.dtype)


def ssd_scan(xs, bc, proj3, colpack, rowg, rowdt, tot_x, skip_x, norm_g):
    nseq, l, _ = xs.shape
    nchunks = l // CHUNK
    cps = SSD_CPS if nchunks % SSD_CPS == 0 else SUBLANES
    assert nchunks % cps == 0, "sequence length must be a multiple of 8 chunks"
    ncs = nchunks // cps
    rows = cps * CHUNK
    blk = lambda ph, cs: jnp.where(ph == 0, ncs - 1 - cs, cs)
    fwd_only = lambda ph, cs: jnp.where(ph == 0, 0, cs)
    zcol = OFF_Z // GROUP_W
    row_spec = lambda d: pl.BlockSpec(
        (1, cps, HEADS_PER_GROUP, CHUNK), lambda b, g, ph, cs: (b, blk(ph, cs), d * SSD_GROUPS + g, 0))
    return pl.pallas_call(
        _ssd_kernel,
        grid=(nseq, SSD_GROUPS, 2, ncs),
        in_specs=[
            pl.BlockSpec((1, rows, GROUP_W), lambda b, g, ph, cs: (b, blk(ph, cs), g)),
            pl.BlockSpec((1, rows, D_STATE), lambda b, g, ph, cs: (b, blk(ph, cs), g)),
            pl.BlockSpec((1, rows, D_STATE), lambda b, g, ph, cs: (b, fwd_only(ph, cs), SSD_GROUPS + g)),
            pl.BlockSpec((1, rows, GROUP_W), lambda b, g, ph, cs: (b, fwd_only(ph, cs), zcol + g)),
            pl.BlockSpec((1, 1, rows, LANES), lambda b, g, ph, cs: (g, b, blk(ph, cs), 0)),
            row_spec(0), row_spec(1), row_spec(0),
            pl.BlockSpec((2, cps, GROUP_W), lambda b, g, ph, cs: (0, b * ncs + blk(ph, cs), g)),
            pl.BlockSpec((1, GROUP_W), lambda b, g, ph, cs: (0, g)),
            pl.BlockSpec((1, GROUP_W), lambda b, g, ph, cs: (0, g)),
        ],
        out_specs=pl.BlockSpec((1, rows, GROUP_W), lambda b, g, ph, cs: (b, fwd_only(ph, cs), g)),
        out_shape=jax.ShapeDtypeStruct((nseq, l, D_INNER), BF16),
        scratch_shapes=[
            pltpu.VMEM((D_STATE, GROUP_W), F32),
            pltpu.VMEM((D_STATE, GROUP_W), F32),
            pltpu.VMEM((nchunks, D_STATE, GROUP_W), BF16),
        ],
        compiler_params=_params("parallel", "parallel", "arbitrary", "arbitrary"),
        name="ssd_scan",
    )(xs, bc, bc, proj3, colpack, rowg, rowg, rowdt, tot_x, skip_x, norm_g)


def _rope_tables(l):
    half = HEAD_DIM // 2
    inv_freq = ROPE_THETA ** (-jnp.arange(half, dtype=F32) * 2.0 / HEAD_DIM)
    ang = jnp.arange(l, dtype=F32)[:, None] * inv_freq[None, :]
    cos, sin = jnp.cos(ang), jnp.sin(ang)
    return jnp.concatenate([cos, cos], axis=-1), jnp.concatenate([-sin, sin], axis=-1)


def _row_tile(t, want):
    while t % want:
        want //= 2
    return want


def _layer(layer, x_parts, nseq, l, cc, ss, w_in, conv_w, conv_b, a_log, dt_bias, skip_x, ssd_norm, attn_sink,
           w_out_attn, w_out_ssd, w_out, norm_mix, norm_ffn, w_gate_up, w_down):
    t = nseq * l
    nchunks = l // CHUNK
    tm = _row_tile(math.gcd(*[p.shape[0] for p in x_parts]), 1024)
    proj = norm_matmul(x_parts, norm_mix, w_in, layer, tm=tm, tn=IN_TILE, out_dtype=F32)
    proj3 = proj.reshape(nseq, l, N_IN)

    attn = windowed_attention(proj3, attn_sink, cc, ss).reshape(t, ATTN_W)

    xs = conv_silu(proj3, conv_w, conv_b, ch_off=0, n_ch=D_INNER, out_dtype=F32)
    bcm = conv_silu(proj3, conv_w, conv_b, ch_off=D_INNER, n_ch=CONV_CH - D_INNER, out_dtype=BF16)

    colpack, rowg, rowdt, tot = dt_prepare(proj, dt_bias, a_log)
    colpack = colpack.reshape(SSD_GROUPS, nseq, l, LANES)
    rowg = rowg.reshape(nseq, nchunks, DH, CHUNK)
    rowdt = rowdt.reshape(nseq, nchunks, DH, CHUNK)
    ssd = ssd_scan(xs, bcm, proj3, colpack, rowg, rowdt, tot, skip_x, ssd_norm).reshape(t, D_INNER)

    mix = gated_mix(attn, ssd, w_out_attn, w_out_ssd, layer, proj, tm=tm, tn=512)
    x = resid_matmul(mix, w_out, layer, x_parts, tm=tm, tn=1024)
    act = norm_swiglu(x, norm_ffn, w_gate_up, layer, tm=tm, tn=512)
    x = resid_matmul(act, w_down, layer, [x], tm=tm, tn=512)
    return x


def _trunk(xs, w_in, conv_w, conv_b, a_log, dt_bias, d_skip, ssd_norm, attn_sink,
           w_out_attn, w_out_ssd, w_out, norm_mix, norm_ffn, w_gate_up, w_down):
    l = xs[0].shape[1]
    nseq = sum(x.shape[0] for x in xs)
    x_parts = [x.reshape(-1, D_MODEL) for x in xs]
    cc, ss = _rope_tables(l)
    w_in16 = w_in.astype(BF16)
    mats = [w.astype(BF16) for w in (w_out_attn, w_out_ssd, w_out)]
    ffn = [w.astype(BF16) for w in (w_gate_up, w_down)]
    for i in range(DEPTH):
        x_parts = [_layer(
            i, x_parts, nseq, l, cc, ss, w_in16, conv_w[i], conv_b[i][None, :],
            a_log[i].reshape(1, -1), dt_bias[i].reshape(1, -1), jnp.repeat(d_skip[i], SSD_HEAD_DIM)[None, :],
            ssd_norm[i][None, :], attn_sink[i], *mats, norm_mix[i][None, :], norm_ffn[i][None, :], *ffn)]
    return x_parts[0]


def kernel(x_prompt, x_sample, w_in, conv_w, conv_b, a_log, dt_bias, d_skip, ssd_norm, attn_sink,
           w_out_attn, w_out_ssd, w_out, norm_mix, norm_ffn, w_gate_up, w_down, final_norm):
    weights = (w_in, conv_w, conv_b, a_log, dt_bias, d_skip, ssd_norm, attn_sink,
               w_out_attn, w_out_ssd, w_out, norm_mix, norm_ffn, w_gate_up, w_down)
    g = final_norm[None, :]
    outs = []
    if x_prompt.shape[1] == x_sample.shape[1]:
        l = x_prompt.shape[1]
        x = _trunk([x_prompt, x_sample], *weights)
        row0 = 0
        for xin in (x_prompt, x_sample):
            n = xin.shape[0] * l
            outs.append(rmsnorm(x, g, tm=_row_tile(l, 512), row0=row0, nrows=n).reshape(xin.shape))
            row0 += n
    else:
        for xin in (x_prompt, x_sample):
            x = _trunk([xin], *weights)
            outs.append(rmsnorm(x, g, tm=_row_tile(x.shape[0], 512), row0=0, nrows=x.shape[0]).reshape(xin.shape))
    return tuple(outs)
```

```python
import functools
import math

import jax
import jax.numpy as jnp
from jax import lax
from jax.experimental import pallas as pl
from jax.experimental.pallas import tpu as pltpu

D_MODEL = 2048
DEPTH = 4
N_HEADS = 16
N_KV_HEADS = 4
HEAD_DIM = 128
Q_PER_KV = N_HEADS // N_KV_HEADS
ATTN_W = N_HEADS * HEAD_DIM
KV_W = N_KV_HEADS * HEAD_DIM
BLOCK = 128
ROPE_THETA = 10000.0
D_INNER = 2 * D_MODEL
SSD_HEAD_DIM = 64
SSD_HEADS = D_INNER // SSD_HEAD_DIM
SSD_GROUPS = 8
HEADS_PER_GROUP = SSD_HEADS // SSD_GROUPS
GROUP_W = D_INNER // SSD_GROUPS
D_STATE = 128
D_CONV = 5
CHUNK = 128
CONV_CH = D_INNER + 2 * SSD_GROUPS * D_STATE
D_FF = ((8 * D_MODEL // 3 + 255) // 256) * 256
EPS = 1e-6

OFF_Q = 0
OFF_K = OFF_Q + ATTN_W
OFF_V = OFF_K + KV_W
OFF_GATT = OFF_V + KV_W
OFF_GSSD = OFF_GATT + D_MODEL
OFF_Z = OFF_GSSD + D_MODEL
OFF_XBC = OFF_Z + D_INNER
OFF_DT = OFF_XBC + CONV_CH
N_IN = OFF_DT + 2 * SSD_HEADS
IN_TILE = 768
IN_TILE_WIDE = 1280

LANES = 128
SUBLANES = 8
VMEM_LIMIT = 56 * 1024 * 1024
NEG_BIG = -1e30
LOG2E = 1.4426950408889634

F32 = jnp.float32
BF16 = jnp.bfloat16


def _params(*sem):
    return pltpu.CompilerParams(dimension_semantics=sem, vmem_limit_bytes=VMEM_LIMIT)


def _sigmoid(x):
    return 1.0 / (1.0 + jnp.exp(-x))


def _silu(x):
    h = 0.5 * x
    return h + h * jnp.tanh(h)


def _dot(a, b):
    return jnp.dot(a, b, preferred_element_type=F32)


NORM_ROWS = 128


def _part_starts(parts, tm):
    starts, lo = [], 0
    for p in parts:
        assert p.shape[0] % tm == 0, "each part must hold whole row tiles"
        starts.append(lo)
        lo += p.shape[0] // tm
    return starts, lo


def _part_specs(parts, starts, block, col):
    return [pl.BlockSpec(block, lambda i, j, lo=lo, n=p.shape[0] // block[0]: (jnp.clip(i - lo, 0, n - 1), col(j)))
            for p, lo in zip(parts, starts)]


def _pick_part(starts, loads):
    i = pl.program_id(0)
    x = loads[0]()
    for lo, load in zip(starts[1:], loads[1:]):
        x = jnp.where(i >= lo, load(), x)
    return x


def _norm_into(x_refs, starts, g_ref, h_ref):
    g = g_ref[...]

    def body(i, c):
        r = pl.ds(pl.multiple_of(i * NORM_ROWS, NORM_ROWS), NORM_ROWS)
        x = _pick_part(starts, [lambda ref=ref: ref[r, :] for ref in x_refs])
        ms = jnp.mean(x * x, axis=-1, keepdims=True)
        h_ref[r, :] = ((x * lax.rsqrt(ms + EPS)) * g).astype(BF16)
        return c

    lax.fori_loop(0, h_ref.shape[0] // NORM_ROWS, body, 0)


def _norm_matmul_kernel(starts, *refs):
    x_refs, (g_ref, w_ref, o_ref, h_ref) = refs[:len(starts)], refs[len(starts):]

    @pl.when(pl.program_id(1) == 0)
    def _():
        _norm_into(x_refs, starts, g_ref, h_ref)

    o_ref[...] = _dot(h_ref[...], w_ref[...]).astype(o_ref.dtype)


def norm_matmul(parts, g, w, layer, *, tm, tn, out_dtype):
    k = parts[0].shape[1]
    n = w.shape[2]
    starts, tiles = _part_starts(parts, tm)
    return pl.pallas_call(
        functools.partial(_norm_matmul_kernel, starts),
        grid=(tiles, pl.cdiv(n, tn)),
        in_specs=_part_specs(parts, starts, (tm, k), lambda j: 0) + [
            pl.BlockSpec((1, k), lambda i, j: (0, 0)),
            pl.BlockSpec((None, k, tn), lambda i, j: (layer, 0, j)),
        ],
        out_specs=pl.BlockSpec((tm, tn), lambda i, j: (i, j)),
        out_shape=jax.ShapeDtypeStruct((tiles * tm, n), out_dtype),
        scratch_shapes=[pltpu.VMEM((tm, k), BF16)],
        compiler_params=_params("parallel", "arbitrary"),
        name="norm_in_proj",
    )(*parts, g, w)


def _norm_swiglu_kernel(x_ref, g_ref, wg_ref, wu_ref, o_ref, h_ref):
    @pl.when(pl.program_id(1) == 0)
    def _():
        _norm_into([x_ref], [0], g_ref, h_ref)

    h = h_ref[...]
    gt = _dot(h, wg_ref[...])
    up = _dot(h, wu_ref[...])
    o_ref[...] = (_silu(gt) * up).astype(o_ref.dtype)


def norm_swiglu(x, g, w_gate_up, layer, *, tm, tn):
    t, k = x.shape
    f = w_gate_up.shape[2] // 2
    nf = f // tn
    return pl.pallas_call(
        _norm_swiglu_kernel,
        grid=(t // tm, nf),
        in_specs=[
            pl.BlockSpec((tm, k), lambda i, j: (i, 0)),
            pl.BlockSpec((1, k), lambda i, j: (0, 0)),
            pl.BlockSpec((None, k, tn), lambda i, j: (layer, 0, j)),
            pl.BlockSpec((None, k, tn), lambda i, j: (layer, 0, j + nf)),
        ],
        out_specs=pl.BlockSpec((tm, tn), lambda i, j: (i, j)),
        out_shape=jax.ShapeDtypeStruct((t, f), BF16),
        scratch_shapes=[pltpu.VMEM((tm, k), BF16)],
        compiler_params=_params("parallel", "arbitrary"),
        name="norm_swiglu",
    )(x, g, w_gate_up, w_gate_up)


def _resid_matmul_kernel(starts, a_ref, w_ref, *refs):
    x_refs, o_ref = refs[:-1], refs[-1]
    x = _pick_part(starts, [lambda ref=ref: ref[...] for ref in x_refs])
    o_ref[...] = x + _dot(a_ref[...], w_ref[...])


def resid_matmul(a, w, layer, x_parts, *, tm, tn):
    t, k = a.shape
    n = w.shape[2]
    starts, tiles = _part_starts(x_parts, tm)
    assert tiles * tm == t
    return pl.pallas_call(
        functools.partial(_resid_matmul_kernel, starts),
        grid=(tiles, n // tn),
        in_specs=[
            pl.BlockSpec((tm, k), lambda i, j: (i, 0)),
            pl.BlockSpec((None, k, tn), lambda i, j: (layer, 0, j)),
        ] + _part_specs(x_parts, starts, (tm, tn), lambda j: j),
        out_specs=pl.BlockSpec((tm, tn), lambda i, j: (i, j)),
        out_shape=jax.ShapeDtypeStruct((t, n), F32),
        compiler_params=_params("parallel", "arbitrary"),
        name="resid_matmul",
    )(a, w, *x_parts)


def _mix_kernel(attn_ref, ssd_ref, wa_ref, ws_ref, ga_ref, gs_ref, o_ref):
    a = _dot(attn_ref[...], wa_ref[...])
    s = _dot(ssd_ref[...], ws_ref[...])
    o_ref[...] = (_sigmoid(ga_ref[...]) * a + _sigmoid(gs_ref[...]) * s).astype(o_ref.dtype)


def gated_mix(attn, ssd, wa, ws, layer, proj, *, tm, tn):
    t = attn.shape[0]
    n = wa.shape[2]
    return pl.pallas_call(
        _mix_kernel,
        grid=(t // tm, n // tn),
        in_specs=[
            pl.BlockSpec((tm, attn.shape[1]), lambda i, j: (i, 0)),
            pl.BlockSpec((tm, ssd.shape[1]), lambda i, j: (i, 0)),
            pl.BlockSpec((None, wa.shape[1], tn), lambda i, j: (layer, 0, j)),
            pl.BlockSpec((None, ws.shape[1], tn), lambda i, j: (layer, 0, j)),
            pl.BlockSpec((tm, tn), lambda i, j: (i, j + OFF_GATT // tn)),
            pl.BlockSpec((tm, tn), lambda i, j: (i, j + OFF_GSSD // tn)),
        ],
        out_specs=pl.BlockSpec((tm, tn), lambda i, j: (i, j)),
        out_shape=jax.ShapeDtypeStruct((t, n), BF16),
        compiler_params=_params("parallel", "arbitrary"),
        name="gated_mix",
    )(attn, ssd, wa, ws, proj, proj)


def _rmsnorm_kernel(x_ref, g_ref, o_ref):
    x = x_ref[...]
    ms = jnp.mean(x * x, axis=-1, keepdims=True)
    o_ref[...] = (x * lax.rsqrt(ms + EPS)) * g_ref[...]


def rmsnorm(x, g, *, tm, row0, nrows):
    k = x.shape[1]
    off = row0 // tm
    return pl.pallas_call(
        _rmsnorm_kernel,
        grid=(nrows // tm,),
        in_specs=[pl.BlockSpec((tm, k), lambda i: (i + off, 0)), pl.BlockSpec((1, k), lambda i: (0, 0))],
        out_specs=pl.BlockSpec((tm, k), lambda i: (i, 0)),
        out_shape=jax.ShapeDtypeStruct((nrows, k), F32),
        compiler_params=_params("parallel"),
        name="final_rmsnorm",
    )(x, g)


def _rope(x, cc, ss):
    return x * cc + pltpu.roll(x, HEAD_DIM // 2, 1) * ss


def _attn_kernel(sink_ref, q_ref, kp_ref, kc_ref, kn_ref, vp_ref, vc_ref, vn_ref, cc_ref, ss_ref, o_ref):
    n = pl.program_id(1)
    nb = pl.num_programs(1)
    pos_c = pl.multiple_of(n * BLOCK, BLOCK)
    pos_p = pl.multiple_of(jnp.maximum(n - 1, 0) * BLOCK, BLOCK)
    pos_n = pl.multiple_of(jnp.minimum(n + 1, nb - 1) * BLOCK, BLOCK)
    cc_c, ss_c = cc_ref[pl.ds(pos_c, BLOCK), :], ss_ref[pl.ds(pos_c, BLOCK), :]
    cc_k = jnp.concatenate([cc_ref[pl.ds(pos_p, BLOCK), :], cc_c, cc_ref[pl.ds(pos_n, BLOCK), :]], axis=0)
    ss_k = jnp.concatenate([ss_ref[pl.ds(pos_p, BLOCK), :], ss_c, ss_ref[pl.ds(pos_n, BLOCK), :]], axis=0)
    cc_q = jnp.concatenate([cc_c] * Q_PER_KV, axis=0)
    ss_q = jnp.concatenate([ss_c] * Q_PER_KV, axis=0)

    rows = Q_PER_KV * BLOCK
    t = lax.broadcasted_iota(jnp.int32, (rows, BLOCK), 0) % BLOCK
    sk = lax.broadcasted_iota(jnp.int32, (rows, BLOCK), 1)
    valid_prev = sk >= t + jnp.where(n > 0, 0, BLOCK)
    valid_next = sk <= t - jnp.where(n < nb - 1, 0, BLOCK)
    scale2 = HEAD_DIM ** -0.5 * LOG2E

    for hk in range(N_KV_HEADS):
        c = slice(hk * HEAD_DIM, (hk + 1) * HEAD_DIM)
        kwin = jnp.concatenate([kp_ref[0, :, c], kc_ref[0, :, c], kn_ref[0, :, c]], axis=0)
        vwin = jnp.concatenate([vp_ref[0, :, c], vc_ref[0, :, c], vn_ref[0, :, c]], axis=0).astype(BF16)
        kr = _rope(kwin, cc_k, ss_k).astype(BF16)
        qg = jnp.concatenate(
            [q_ref[0, :, (hk * Q_PER_KV + g) * HEAD_DIM:(hk * Q_PER_KV + g + 1) * HEAD_DIM] for g in range(Q_PER_KV)],
            axis=0,
        )
        qr = _rope(qg, cc_q, ss_q).astype(BF16)
        s = lax.dot_general(qr, kr, (((1,), (1,)), ((), ())), preferred_element_type=F32) * scale2
        s = jnp.concatenate([jnp.where(valid_prev, s[:, :BLOCK], NEG_BIG), s[:, BLOCK:2 * BLOCK],
                             jnp.where(valid_next, s[:, 2 * BLOCK:], NEG_BIG)], axis=1)
        ps, dens = [], []
        for g in range(Q_PER_KV):
            sg = s[g * BLOCK:(g + 1) * BLOCK]
            sink2 = sink_ref[hk * Q_PER_KV + g] * LOG2E
            m = jnp.maximum(jnp.max(sg, axis=-1, keepdims=True), sink2)
            p = jnp.exp2(sg - m)
            dens.append(jnp.sum(p, axis=-1, keepdims=True) + jnp.exp2(sink2 - m))
            ps.append(p.astype(BF16))
        pv = _dot(jnp.concatenate(ps, axis=0), vwin)
        for g in range(Q_PER_KV):
            h = hk * Q_PER_KV + g
            o = pv[g * BLOCK:(g + 1) * BLOCK] / dens[g]
            o_ref[0, :, h * HEAD_DIM:(h + 1) * HEAD_DIM] = o.astype(o_ref.dtype)


def windowed_attention(proj3, sink, cc, ss):
    nseq, l, _ = proj3.shape
    nb = l // BLOCK
    kcol, vcol = OFF_K // KV_W, OFF_V // KV_W
    prev = lambda n: jnp.maximum(n - 1, 0)
    nxt = lambda n: jnp.minimum(n + 1, nb - 1)
    kv = lambda rowf, col: pl.BlockSpec((1, BLOCK, KV_W), lambda b, n: (b, rowf(n), col))
    ident = lambda n: n
    return pl.pallas_call(
        _attn_kernel,
        grid=(nseq, nb),
        in_specs=[
            pl.BlockSpec(memory_space=pltpu.SMEM),
            pl.BlockSpec((1, BLOCK, ATTN_W), lambda b, n: (b, n, OFF_Q // ATTN_W)),
            kv(prev, kcol), kv(ident, kcol), kv(nxt, kcol),
            kv(prev, vcol), kv(ident, vcol), kv(nxt, vcol),
            pl.BlockSpec((l, HEAD_DIM), lambda b, n: (0, 0)),
            pl.BlockSpec((l, HEAD_DIM), lambda b, n: (0, 0)),
        ],
        out_specs=pl.BlockSpec((1, BLOCK, ATTN_W), lambda b, n: (b, n, 0)),
        out_shape=jax.ShapeDtypeStruct((nseq, l, ATTN_W), BF16),
        compiler_params=_params("parallel", "arbitrary"),
        name="windowed_attention",
    )(sink, proj3, proj3, proj3, proj3, proj3, proj3, proj3, cc, ss)


HALO = SUBLANES
PAD = D_CONV // 2
CONV_PHASES = 4
CONV_SUB = 64
CONV_BLK = CONV_PHASES * CONV_SUB


def _conv_kernel(x_ref, w_ref, b_ref, o_ref, edge_ref, *res_ref):
    rows = x_ref.shape[1]
    nblk = rows // CONV_BLK
    w = [w_ref[j:j + 1, :] for j in range(D_CONV)]
    bias = b_ref[...]

    def emit(tap, out_base):
        for r in range(CONV_PHASES):
            acc = bias
            for j in range(D_CONV):
                acc = acc + tap(r + j - PAD) * w[j]
            y = _silu(acc)
            if res_ref:
                res_ref[0][pl.ds(out_base + r, CONV_SUB, stride=CONV_PHASES), :] = y
            else:
                o_ref[0, pl.ds(out_base + r, CONV_SUB, stride=CONV_PHASES), :] = y

    def body(i, c):
        base = i * CONV_BLK
        emit(lambda off: x_ref[0, pl.ds(base + off, CONV_SUB, stride=CONV_PHASES), :], base)
        return c

    lax.fori_loop(1, nblk - 1, body, 0)

    zeros = jnp.zeros((HALO, LANES), F32)
    edge_tap = lambda slot: (lambda off: edge_ref[slot, pl.ds(HALO + off, CONV_SUB, stride=CONV_PHASES), :])
    edge_ref[0, 0:HALO, :] = zeros
    if nblk == 1:
        edge_ref[0, HALO:HALO + rows, :] = x_ref[0]
        edge_ref[0, HALO + rows:, :] = zeros
        emit(edge_tap(0), 0)
    else:
        edge_ref[0, HALO:, :] = x_ref[0, 0:CONV_BLK + HALO, :]
        edge_ref[1, 0:HALO + CONV_BLK, :] = x_ref[0, rows - CONV_BLK - HALO:rows, :]
        edge_ref[1, HALO + CONV_BLK:, :] = zeros
        emit(edge_tap(0), 0)
        emit(edge_tap(1), rows - CONV_BLK)
    if res_ref:
        o_ref[0] = res_ref[0][...].astype(o_ref.dtype)


def conv_silu(proj3, conv_w, conv_b, *, ch_off, n_ch, out_dtype):
    nseq, l, _ = proj3.shape
    col0 = (OFF_XBC + ch_off) // LANES
    wcol0 = ch_off // LANES
    scratch = [pltpu.VMEM((2, CONV_BLK + 2 * HALO, LANES), F32)]
    if out_dtype != F32:
        scratch.append(pltpu.VMEM((l, LANES), F32))
    return pl.pallas_call(
        _conv_kernel,
        grid=(nseq, n_ch // LANES),
        in_specs=[
            pl.BlockSpec((1, l, LANES), lambda b, c: (b, 0, col0 + c)),
            pl.BlockSpec((D_CONV, LANES), lambda b, c: (0, wcol0 + c)),
            pl.BlockSpec((1, LANES), lambda b, c: (0, wcol0 + c)),
        ],
        out_specs=pl.BlockSpec((1, l, LANES), lambda b, c: (b, 0, c)),
        out_shape=jax.ShapeDtypeStruct((nseq, l, n_ch), out_dtype),
        scratch_shapes=scratch,
        compiler_params=_params("parallel", "parallel"),
        name="conv_silu",
    )(proj3, conv_w, conv_b)


DT_CHUNKS = 8
PAIRS = HEADS_PER_GROUP // 2
DH = 2 * SSD_HEADS
COL_AF, COL_AB, COL_GF, COL_GB = (k * HEADS_PER_GROUP for k in range(4))
COL_USED = 4 * HEADS_PER_GROUP
GROUP_LANES = SSD_HEADS // 2


def _split3(a):
    hi = a.astype(BF16)
    r = a - hi.astype(F32)
    mid = r.astype(BF16)
    lo = (r - mid.astype(F32)).astype(BF16)
    return hi, mid, lo


def _tri_sum(tri, a):
    hi, mid, lo = _split3(a)
    return _dot(tri, hi) + _dot(tri, mid) + _dot(tri, lo)


def _pair_rows(v, row_lo, lane_even):
    swap = lambda u: jnp.concatenate([u[CHUNK // 2:], u[:CHUNK // 2]], axis=0)
    from_prev_lane = swap(pltpu.roll(v, 1, 1))
    from_next_lane = swap(pltpu.roll(v, DH - 1, 1))
    w = jnp.where(row_lo, jnp.where(lane_even, v, from_prev_lane), jnp.where(lane_even, from_next_lane, v))
    return w.T


def _dt_kernel(raw_ref, bias_ref, alog_ref, col_ref, rowg_ref, rowdt_ref, tot_ref):
    neg_a2 = -jnp.exp(alog_ref[...]) * LOG2E
    bias = bias_ref[...]
    r = lax.broadcasted_iota(jnp.int32, (CHUNK, CHUNK), 0)
    c = lax.broadcasted_iota(jnp.int32, (CHUNK, CHUNK), 1)
    lower = (r >= c).astype(BF16)
    upper = (r <= c).astype(BF16)
    fwd_lane = c < SSD_HEADS
    row_lo = r < CHUNK // 2
    lane_even = (c & 1) == 0
    src0 = ((c >> 3) & 1) * SSD_HEADS + (c & (HEADS_PER_GROUP - 1))
    low_groups = (c & GROUP_LANES) == 0
    lane8 = lax.broadcasted_iota(jnp.int32, (DT_CHUNKS, DH), 1)
    totals = []
    for ci in range(DT_CHUNKS):
        rows = slice(ci * CHUNK, (ci + 1) * CHUNK)
        raw = raw_ref[rows, :] + bias
        dt = jnp.maximum(raw, 0.0) + jnp.log1p(jnp.exp(-jnp.abs(raw)))
        a2 = dt * neg_a2
        pre = _tri_sum(lower, a2)
        suf = _tri_sum(upper, a2)
        acum2 = jnp.where(fwd_lane, pre, suf)
        g = jnp.log2(dt) - acum2
        totals.append(jnp.where(fwd_lane[0:1], pre[CHUNK - 1:CHUNK], suf[0:1]))
        rowg_ref[ci] = _pair_rows(g, row_lo, lane_even)
        rowdt_ref[ci] = _pair_rows(dt, row_lo, lane_even)
        sources = (jnp.where(low_groups, acum2, pltpu.roll(g, GROUP_LANES, 1)),
                   jnp.where(low_groups, pltpu.roll(g, DH - GROUP_LANES, 1), acum2))
        for grp in range(SSD_GROUPS):
            hi = grp >= SSD_GROUPS // 2
            idx = src0 + grp * HEADS_PER_GROUP + jnp.where(c < COL_GF, 0, -GROUP_LANES if hi else GROUP_LANES)
            packed = jnp.take_along_axis(sources[hi], idx, axis=1)
            col_ref[grp, rows, :] = jnp.where(c < COL_USED, packed, 0.0)
    tot8 = jnp.concatenate(totals, axis=0)
    for d in range(2):
        for m in range(SSD_HEADS // 2):
            idx = jnp.where(lane8 < SSD_HEAD_DIM, d * SSD_HEADS + 2 * m, d * SSD_HEADS + 2 * m + 1)
            tot_ref[d, :, m * LANES:(m + 1) * LANES] = jnp.take_along_axis(tot8, idx, axis=1)


def dt_prepare(proj, dt_bias, a_log):
    t = proj.shape[0]
    rows = DT_CHUNKS * CHUNK
    assert t % rows == 0, "token count must be a multiple of DT_CHUNKS chunks"
    cps = DT_CHUNKS
    nchunks = t // CHUNK
    col = OFF_DT // DH
    return pl.pallas_call(
        _dt_kernel,
        grid=(t // rows,),
        in_specs=[
            pl.BlockSpec((rows, DH), lambda i: (i, col)),
            pl.BlockSpec((1, DH), lambda i: (0, 0)),
            pl.BlockSpec((1, DH), lambda i: (0, 0)),
        ],
        out_specs=[
            pl.BlockSpec((SSD_GROUPS, rows, LANES), lambda i: (0, i, 0)),
            pl.BlockSpec((cps, DH, CHUNK), lambda i: (i, 0, 0)),
            pl.BlockSpec((cps, DH, CHUNK), lambda i: (i, 0, 0)),
            pl.BlockSpec((2, cps, D_INNER), lambda i: (0, i, 0)),
        ],
        out_shape=[
            jax.ShapeDtypeStruct((SSD_GROUPS, t, LANES), F32),
            jax.ShapeDtypeStruct((nchunks, DH, CHUNK), F32),
            jax.ShapeDtypeStruct((nchunks, DH, CHUNK), F32),
            jax.ShapeDtypeStruct((2, nchunks, D_INNER), F32),
        ],
        compiler_params=_params("parallel"),
        name="dt_prepare",
    )(proj, dt_bias, a_log)


SSD_CPS = 16
HALF = CHUNK // 2


def _expand(colv, base, lane_lo):
    tiles = []
    for p in range(PAIRS):
        idx = jnp.where(lane_lo, base + 2 * p, base + 2 * p + 1)
        tiles.append(jnp.take_along_axis(colv, idx, axis=1))
    return jnp.concatenate(tiles, axis=1)


def _ssd_kernel(x_ref, b_ref, c_ref, z_ref, col_ref, rowgf_ref, rowgb_ref, rowdtf_ref,
                tot_ref, skip_ref, ng_ref, o_ref, sf_ref, sb_ref, sb_all_ref):
    ph = pl.program_id(2)
    cs = pl.program_id(3)
    ncs = pl.num_programs(3)
    cps = x_ref.shape[1] // CHUNK
    lane = lax.broadcasted_iota(jnp.int32, (CHUNK, LANES), 1)
    lane_lo = lane < SSD_HEAD_DIM

    def state_update(s_ref, xs, bc, g_x, tot_row):
        xw = (xs * jnp.exp2(tot_row + g_x)).astype(BF16)
        st = lax.dot_general(bc, xw, (((0,), (0,)), ((), ())), preferred_element_type=F32)
        s_ref[...] = s_ref[...] * jnp.exp2(tot_row) + st

    @pl.when(ph == 0)
    def _backward_states():
        @pl.when(cs == 0)
        def _():
            sb_ref[...] = jnp.zeros_like(sb_ref)

        for k in range(cps):
            ci = cps - 1 - k
            rows = slice(ci * CHUNK, (ci + 1) * CHUNK)
            chunk = (ncs - 1 - cs) * cps + ci
            sb_all_ref[chunk] = sb_ref[...].astype(BF16)
            gb = _expand(col_ref[0, 0, rows, :], COL_GB, lane_lo)
            state_update(sb_ref, x_ref[0, rows, :], b_ref[0, rows, :], gb, tot_ref[1, ci:ci + 1, :])

    @pl.when(ph == 1)
    def _forward_and_output():
        @pl.when(cs == 0)
        def _():
            sf_ref[...] = jnp.zeros_like(sf_ref)

        li = lax.broadcasted_iota(jnp.int32, (HALF, LANES), 0)
        s_in_half = lax.broadcasted_iota(jnp.int32, (HALF, LANES), 1) & (HALF - 1)
        fwd = li > s_in_half
        diag = li == s_in_half
        for ci in range(cps):
            rows = slice(ci * CHUNK, (ci + 1) * CHUNK)
            chunk = cs * cps + ci
            xs = x_ref[0, rows, :]
            xs16 = xs.astype(BF16)
            bc = b_ref[0, rows, :]
            cc = c_ref[0, rows, :]
            cb = lax.dot_general(cc, bc, (((1,), (1,)), ((), ())), preferred_element_type=F32)
            cb_rot = pltpu.roll(cb, HALF, 1)
            cbs = [jnp.where(lane_lo, cb, cb_rot), jnp.where(lane_lo, cb_rot, cb)]
            colv = col_ref[0, 0, rows, :]
            af = _expand(colv, COL_AF, lane_lo)
            ab = _expand(colv, COL_AB, lane_lo)
            gf = _expand(colv, COL_GF, lane_lo)

            ydiag = []
            for p in range(PAIRS):
                afp = af[:, p * LANES:(p + 1) * LANES]
                abp = ab[:, p * LANES:(p + 1) * LANES]
                tiles = []
                for half in (0, 1):
                    rr = slice(2 * p + half, 2 * p + half + 1)
                    rgf, rgb = rowgf_ref[0, ci, rr, :], rowgb_ref[0, ci, rr, :]
                    same = slice(half * HALF, (half + 1) * HALF)
                    e_same = jnp.exp2(jnp.where(fwd, afp[same] + rgf, abp[same] + rgb))
                    e_same = e_same + jnp.where(diag, rowdtf_ref[0, ci, rr, :], 0.0)
                    if half == 0:
                        e = jnp.concatenate([e_same, jnp.exp2(afp[HALF:] + rgf)], axis=0)
                    else:
                        e = jnp.concatenate([jnp.exp2(abp[:HALF] + rgb), e_same], axis=0)
                    tiles.append((cbs[half] * e).astype(BF16))
                xp = xs16[:, p * LANES:(p + 1) * LANES]
                zero = jnp.zeros_like(xp)
                x_lo = jnp.where(lane_lo, xp, zero)
                x_hi = jnp.where(lane_lo, zero, xp)
                rhs = jnp.concatenate([x_lo[:HALF], x_hi[:HALF], x_lo[HALF:], x_hi[HALF:]], axis=0)
                ydiag.append(_dot(jnp.concatenate(tiles, axis=1), rhs))
            y = jnp.concatenate(ydiag, axis=1)

            y = y + _dot(cc, sf_ref[...].astype(BF16)) * jnp.exp2(af)
            y = y + _dot(cc, sb_all_ref[chunk]) * jnp.exp2(ab)
            y = y + xs * skip_ref[...]

            state_update(sf_ref, xs, bc, gf, tot_ref[0, ci:ci + 1, :])

            z = z_ref[0, rows, :]
            y = y * _silu(z)
            ms = jnp.mean(y * y, axis=-1, keepdims=True)
            o_ref[0, rows, :] = ((y * lax.rsqrt(ms + EPS)) * ng_ref[...]).astype(o_ref.dtype)


def ssd_scan(xs, bc, proj3, colpack, rowg, rowdt, tot_x, skip_x, norm_g):
    nseq, l, _ = xs.shape
    nchunks = l // CHUNK
    cps = SSD_CPS if nchunks % SSD_CPS == 0 else SUBLANES
    assert nchunks % cps == 0, "sequence length must be a multiple of 8 chunks"
    ncs = nchunks // cps
    rows = cps * CHUNK
    blk = lambda ph, cs: jnp.where(ph == 0, ncs - 1 - cs, cs)
    fwd_only = lambda ph, cs: jnp.where(ph == 0, 0, cs)
    zcol = OFF_Z // GROUP_W
    row_spec = lambda d: pl.BlockSpec(
        (1, cps, HEADS_PER_GROUP, CHUNK), lambda b, g, ph, cs: (b, blk(ph, cs), d * SSD_GROUPS + g, 0))
    return pl.pallas_call(
        _ssd_kernel,
        grid=(nseq, SSD_GROUPS, 2, ncs),
        in_specs=[
            pl.BlockSpec((1, rows, GROUP_W), lambda b, g, ph, cs: (b, blk(ph, cs), g)),
            pl.BlockSpec((1, rows, D_STATE), lambda b, g, ph, cs: (b, blk(ph, cs), g)),
            pl.BlockSpec((1, rows, D_STATE), lambda b, g, ph, cs: (b, fwd_only(ph, cs), SSD_GROUPS + g)),
            pl.BlockSpec((1, rows, GROUP_W), lambda b, g, ph, cs: (b, fwd_only(ph, cs), zcol + g)),
            pl.BlockSpec((1, 1, rows, LANES), lambda b, g, ph, cs: (g, b, blk(ph, cs), 0)),
            row_spec(0), row_spec(1), row_spec(0),
            pl.BlockSpec((2, cps, GROUP_W), lambda b, g, ph, cs: (0, b * ncs + blk(ph, cs), g)),
            pl.BlockSpec((1, GROUP_W), lambda b, g, ph, cs: (0, g)),
            pl.BlockSpec((1, GROUP_W), lambda b, g, ph, cs: (0, g)),
        ],
        out_specs=pl.BlockSpec((1, rows, GROUP_W), lambda b, g, ph, cs: (b, fwd_only(ph, cs), g)),
        out_shape=jax.ShapeDtypeStruct((nseq, l, D_INNER), BF16),
        scratch_shapes=[
            pltpu.VMEM((D_STATE, GROUP_W), F32),
            pltpu.VMEM((D_STATE, GROUP_W), F32),
            pltpu.VMEM((nchunks, D_STATE, GROUP_W), BF16),
        ],
        compiler_params=_params("parallel", "parallel", "arbitrary", "arbitrary"),
        name="ssd_scan",
    )(xs, bc, bc, proj3, colpack, rowg, rowg, rowdt, tot_x, skip_x, norm_g)


def _rope_tables(l):
    half = HEAD_DIM // 2
    inv_freq = ROPE_THETA ** (-jnp.arange(half, dtype=F32) * 2.0 / HEAD_DIM)
    ang = jnp.arange(l, dtype=F32)[:, None] * inv_freq[None, :]
    cos, sin = jnp.cos(ang), jnp.sin(ang)
    return jnp.concatenate([cos, cos], axis=-1), jnp.concatenate([-sin, sin], axis=-1)


def _row_tile(t, want):
    while t % want:
        want //= 2
    return want


def _layer(layer, x_parts, nseq, l, cc, ss, w_in, conv_w, conv_b, a_log, dt_bias, skip_x, ssd_norm, attn_sink,
           w_out_attn, w_out_ssd, w_out, norm_mix, norm_ffn, w_gate_up, w_down):
    t = nseq * l
    nchunks = l // CHUNK
    tm = _row_tile(math.gcd(*[p.shape[0] for p in x_parts]), 1024)
    in_tile = IN_TILE if len(x_parts) > 1 else IN_TILE_WIDE
    proj = norm_matmul(x_parts, norm_mix, w_in, layer, tm=tm, tn=in_tile, out_dtype=F32)
    proj3 = proj.reshape(nseq, l, N_IN)

    attn = windowed_attention(proj3, attn_sink, cc, ss).reshape(t, ATTN_W)

    xs = conv_silu(proj3, conv_w, conv_b, ch_off=0, n_ch=D_INNER, out_dtype=F32)
    bcm = conv_silu(proj3, conv_w, conv_b, ch_off=D_INNER, n_ch=CONV_CH - D_INNER, out_dtype=BF16)

    colpack, rowg, rowdt, tot = dt_prepare(proj, dt_bias, a_log)
    colpack = colpack.reshape(SSD_GROUPS, nseq, l, LANES)
    rowg = rowg.reshape(nseq, nchunks, DH, CHUNK)
    rowdt = rowdt.reshape(nseq, nchunks, DH, CHUNK)
    ssd = ssd_scan(xs, bcm, proj3, colpack, rowg, rowdt, tot, skip_x, ssd_norm).reshape(t, D_INNER)

    mix = gated_mix(attn, ssd, w_out_attn, w_out_ssd, layer, proj, tm=tm, tn=512)
    x = resid_matmul(mix, w_out, layer, x_parts, tm=tm, tn=1024)
    act = norm_swiglu(x, norm_ffn, w_gate_up, layer, tm=tm, tn=512)
    x = resid_matmul(act, w_down, layer, [x], tm=tm, tn=512)
    return x


def _trunk(xs, w_in, conv_w, conv_b, a_log, dt_bias, d_skip, ssd_norm, attn_sink,
           w_out_attn, w_out_ssd, w_out, norm_mix, norm_ffn, w_gate_up, w_down):
    l = xs[0].shape[1]
    nseq = sum(x.shape[0] for x in xs)
    x_parts = [x.reshape(-1, D_MODEL) for x in xs]
    cc, ss = _rope_tables(l)
    w_in16 = w_in.astype(BF16)
    mats = [w.astype(BF16) for w in (w_out_attn, w_out_ssd, w_out)]
    ffn = [w.astype(BF16) for w in (w_gate_up, w_down)]
    for i in range(DEPTH):
        x_parts = [_layer(
            i, x_parts, nseq, l, cc, ss, w_in16, conv_w[i], conv_b[i][None, :],
            a_log[i].reshape(1, -1), dt_bias[i].reshape(1, -1), jnp.repeat(d_skip[i], SSD_HEAD_DIM)[None, :],
            ssd_norm[i][None, :], attn_sink[i], *mats, norm_mix[i][None, :], norm_ffn[i][None, :], *ffn)]
    return x_parts[0]


def kernel(x_prompt, x_sample, w_in, conv_w, conv_b, a_log, dt_bias, d_skip, ssd_norm, attn_sink,
           w_out_attn, w_out_ssd, w_out, norm_mix, norm_ffn, w_gate_up, w_down, final_norm):
    weights = (w_in, conv_w, conv_b, a_log, dt_bias, d_skip, ssd_norm, attn_sink,
               w_out_attn, w_out_ssd, w_out, norm_mix, norm_ffn, w_gate_up, w_down)
    g = final_norm[None, :]
    outs = []
    if x_prompt.shape[1] == x_sample.shape[1]:
        l = x_prompt.shape[1]
        x = _trunk([x_prompt, x_sample], *weights)
        row0 = 0
        for xin in (x_prompt, x_sample):
            n = xin.shape[0] * l
            outs.append(rmsnorm(x, g, tm=_row_tile(l, 512), row0=row0, nrows=n).reshape(xin.shape))
            row0 += n
    else:
        for xin in (x_prompt, x_sample):
            x = _trunk([xin], *weights)
            outs.append(rmsnorm(x, g, tm=_row_tile(x.shape[0], 512), row0=0, nrows=x.shape[0]).reshape(xin.shape))
    return tuple(outs)
```

```python
import functools
import math

import jax
import jax.numpy as jnp
from jax import lax
from jax.experimental import pallas as pl
from jax.experimental.pallas import tpu as pltpu

D_MODEL = 2048
DEPTH = 4
N_HEADS = 16
N_KV_HEADS = 4
HEAD_DIM = 128
Q_PER_KV = N_HEADS // N_KV_HEADS
ATTN_W = N_HEADS * HEAD_DIM
KV_W = N_KV_HEADS * HEAD_DIM
BLOCK = 128
ROPE_THETA = 10000.0
D_INNER = 2 * D_MODEL
SSD_HEAD_DIM = 64
SSD_HEADS = D_INNER // SSD_HEAD_DIM
SSD_GROUPS = 8
HEADS_PER_GROUP = SSD_HEADS // SSD_GROUPS
GROUP_W = D_INNER // SSD_GROUPS
D_STATE = 128
D_CONV = 5
CHUNK = 128
CONV_CH = D_INNER + 2 * SSD_GROUPS * D_STATE
D_FF = ((8 * D_MODEL // 3 + 255) // 256) * 256
EPS = 1e-6

OFF_Q = 0
OFF_K = OFF_Q + ATTN_W
OFF_V = OFF_K + KV_W
OFF_GATT = OFF_V + KV_W
OFF_GSSD = OFF_GATT + D_MODEL
OFF_Z = OFF_GSSD + D_MODEL
OFF_XBC = OFF_Z + D_INNER
OFF_DT = OFF_XBC + CONV_CH
N_IN = OFF_DT + 2 * SSD_HEADS
IN_TILE_WIDE = 1280

LANES = 128
SUBLANES = 8
VMEM_LIMIT = 56 * 1024 * 1024
NEG_BIG = -1e30
LOG2E = 1.4426950408889634

F32 = jnp.float32
BF16 = jnp.bfloat16


def _params(*sem):
    return pltpu.CompilerParams(dimension_semantics=sem, vmem_limit_bytes=VMEM_LIMIT)


def _sigmoid(x):
    return 1.0 / (1.0 + jnp.exp(-x))


def _silu(x):
    h = 0.5 * x
    return h + h * jnp.tanh(h)


def _dot(a, b):
    return jnp.dot(a, b, preferred_element_type=F32)


NORM_ROWS = 128


def _part_starts(parts, tm):
    starts, lo = [], 0
    for p in parts:
        assert p.shape[0] % tm == 0, "each part must hold whole row tiles"
        starts.append(lo)
        lo += p.shape[0] // tm
    return starts, lo


def _part_specs(parts, starts, block, col, single_buffer=False):
    extra = dict(pipeline_mode=pl.Buffered(1)) if single_buffer else {}
    return [pl.BlockSpec(block, lambda i, j, lo=lo, n=p.shape[0] // block[0]: (jnp.clip(i - lo, 0, n - 1), col(j)),
                         **extra)
            for p, lo in zip(parts, starts)]


def _pick_part(starts, loads):
    i = pl.program_id(0)
    x = loads[0]()
    for lo, load in zip(starts[1:], loads[1:]):
        x = jnp.where(i >= lo, load(), x)
    return x


def _norm_into(x_refs, starts, g_ref, h_ref):
    g = g_ref[...]

    def body(i, c):
        r = pl.ds(pl.multiple_of(i * NORM_ROWS, NORM_ROWS), NORM_ROWS)
        x = _pick_part(starts, [lambda ref=ref: ref[r, :] for ref in x_refs])
        ms = jnp.mean(x * x, axis=-1, keepdims=True)
        h_ref[r, :] = ((x * lax.rsqrt(ms + EPS)) * g).astype(BF16)
        return c

    lax.fori_loop(0, h_ref.shape[0] // NORM_ROWS, body, 0)


def _norm_matmul_kernel(starts, *refs):
    x_refs, (g_ref, w_ref, o_ref, h_ref) = refs[:len(starts)], refs[len(starts):]

    @pl.when(pl.program_id(1) == 0)
    def _():
        _norm_into(x_refs, starts, g_ref, h_ref)

    o_ref[...] = _dot(h_ref[...], w_ref[...]).astype(o_ref.dtype)


def norm_matmul(parts, g, w, layer, *, tm, tn, out_dtype):
    k = parts[0].shape[1]
    n = w.shape[2]
    starts, tiles = _part_starts(parts, tm)
    return pl.pallas_call(
        functools.partial(_norm_matmul_kernel, starts),
        grid=(tiles, pl.cdiv(n, tn)),
        in_specs=_part_specs(parts, starts, (tm, k), lambda j: 0, single_buffer=len(parts) > 1) + [
            pl.BlockSpec((1, k), lambda i, j: (0, 0)),
            pl.BlockSpec((None, k, tn), lambda i, j: (layer, 0, j)),
        ],
        out_specs=pl.BlockSpec((tm, tn), lambda i, j: (i, j)),
        out_shape=jax.ShapeDtypeStruct((tiles * tm, n), out_dtype),
        scratch_shapes=[pltpu.VMEM((tm, k), BF16)],
        compiler_params=_params("parallel", "arbitrary"),
        name="norm_in_proj",
    )(*parts, g, w)


def _norm_swiglu_kernel(x_ref, g_ref, wg_ref, wu_ref, o_ref, h_ref):
    @pl.when(pl.program_id(1) == 0)
    def _():
        _norm_into([x_ref], [0], g_ref, h_ref)

    h = h_ref[...]
    gt = _dot(h, wg_ref[...])
    up = _dot(h, wu_ref[...])
    o_ref[...] = (_silu(gt) * up).astype(o_ref.dtype)


def norm_swiglu(x, g, w_gate_up, layer, *, tm, tn):
    t, k = x.shape
    f = w_gate_up.shape[2] // 2
    nf = f // tn
    return pl.pallas_call(
        _norm_swiglu_kernel,
        grid=(t // tm, nf),
        in_specs=[
            pl.BlockSpec((tm, k), lambda i, j: (i, 0)),
            pl.BlockSpec((1, k), lambda i, j: (0, 0)),
            pl.BlockSpec((None, k, tn), lambda i, j: (layer, 0, j)),
            pl.BlockSpec((None, k, tn), lambda i, j: (layer, 0, j + nf)),
        ],
        out_specs=pl.BlockSpec((tm, tn), lambda i, j: (i, j)),
        out_shape=jax.ShapeDtypeStruct((t, f), BF16),
        scratch_shapes=[pltpu.VMEM((tm, k), BF16)],
        compiler_params=_params("parallel", "arbitrary"),
        name="norm_swiglu",
    )(x, g, w_gate_up, w_gate_up)


def _resid_matmul_kernel(starts, a_ref, w_ref, *refs):
    x_refs, o_ref = refs[:-1], refs[-1]
    x = _pick_part(starts, [lambda ref=ref: ref[...] for ref in x_refs])
    o_ref[...] = x + _dot(a_ref[...], w_ref[...])


def resid_matmul(a, w, layer, x_parts, *, tm, tn):
    t, k = a.shape
    n = w.shape[2]
    starts, tiles = _part_starts(x_parts, tm)
    assert tiles * tm == t
    return pl.pallas_call(
        functools.partial(_resid_matmul_kernel, starts),
        grid=(tiles, n // tn),
        in_specs=[
            pl.BlockSpec((tm, k), lambda i, j: (i, 0)),
            pl.BlockSpec((None, k, tn), lambda i, j: (layer, 0, j)),
        ] + _part_specs(x_parts, starts, (tm, tn), lambda j: j),
        out_specs=pl.BlockSpec((tm, tn), lambda i, j: (i, j)),
        out_shape=jax.ShapeDtypeStruct((t, n), F32),
        compiler_params=_params("parallel", "arbitrary"),
        name="resid_matmul",
    )(a, w, *x_parts)


def _mix_kernel(attn_ref, ssd_ref, wa_ref, ws_ref, ga_ref, gs_ref, o_ref):
    a = _dot(attn_ref[...], wa_ref[...])
    s = _dot(ssd_ref[...], ws_ref[...])
    o_ref[...] = (_sigmoid(ga_ref[...]) * a + _sigmoid(gs_ref[...]) * s).astype(o_ref.dtype)


def gated_mix(attn, ssd, wa, ws, layer, proj, *, tm, tn):
    t = attn.shape[0]
    n = wa.shape[2]
    return pl.pallas_call(
        _mix_kernel,
        grid=(t // tm, n // tn),
        in_specs=[
            pl.BlockSpec((tm, attn.shape[1]), lambda i, j: (i, 0)),
            pl.BlockSpec((tm, ssd.shape[1]), lambda i, j: (i, 0)),
            pl.BlockSpec((None, wa.shape[1], tn), lambda i, j: (layer, 0, j)),
            pl.BlockSpec((None, ws.shape[1], tn), lambda i, j: (layer, 0, j)),
            pl.BlockSpec((tm, tn), lambda i, j: (i, j + OFF_GATT // tn)),
            pl.BlockSpec((tm, tn), lambda i, j: (i, j + OFF_GSSD // tn)),
        ],
        out_specs=pl.BlockSpec((tm, tn), lambda i, j: (i, j)),
        out_shape=jax.ShapeDtypeStruct((t, n), BF16),
        compiler_params=_params("parallel", "arbitrary"),
        name="gated_mix",
    )(attn, ssd, wa, ws, proj, proj)


def _rmsnorm_kernel(x_ref, g_ref, o_ref):
    x = x_ref[...]
    ms = jnp.mean(x * x, axis=-1, keepdims=True)
    o_ref[...] = (x * lax.rsqrt(ms + EPS)) * g_ref[...]


def rmsnorm(x, g, *, tm, row0, nrows):
    k = x.shape[1]
    off = row0 // tm
    return pl.pallas_call(
        _rmsnorm_kernel,
        grid=(nrows // tm,),
        in_specs=[pl.BlockSpec((tm, k), lambda i: (i + off, 0)), pl.BlockSpec((1, k), lambda i: (0, 0))],
        out_specs=pl.BlockSpec((tm, k), lambda i: (i, 0)),
        out_shape=jax.ShapeDtypeStruct((nrows, k), F32),
        compiler_params=_params("parallel"),
        name="final_rmsnorm",
    )(x, g)


def _rope(x, cc, ss):
    return x * cc + pltpu.roll(x, HEAD_DIM // 2, 1) * ss


def _attn_kernel(sink_ref, q_ref, kp_ref, kc_ref, kn_ref, vp_ref, vc_ref, vn_ref, cc_ref, ss_ref, o_ref):
    n = pl.program_id(1)
    nb = pl.num_programs(1)
    pos_c = pl.multiple_of(n * BLOCK, BLOCK)
    pos_p = pl.multiple_of(jnp.maximum(n - 1, 0) * BLOCK, BLOCK)
    pos_n = pl.multiple_of(jnp.minimum(n + 1, nb - 1) * BLOCK, BLOCK)
    cc_c, ss_c = cc_ref[pl.ds(pos_c, BLOCK), :], ss_ref[pl.ds(pos_c, BLOCK), :]
    cc_k = jnp.concatenate([cc_ref[pl.ds(pos_p, BLOCK), :], cc_c, cc_ref[pl.ds(pos_n, BLOCK), :]], axis=0)
    ss_k = jnp.concatenate([ss_ref[pl.ds(pos_p, BLOCK), :], ss_c, ss_ref[pl.ds(pos_n, BLOCK), :]], axis=0)
    cc_q = jnp.concatenate([cc_c] * Q_PER_KV, axis=0)
    ss_q = jnp.concatenate([ss_c] * Q_PER_KV, axis=0)

    rows = Q_PER_KV * BLOCK
    t = lax.broadcasted_iota(jnp.int32, (rows, BLOCK), 0) % BLOCK
    sk = lax.broadcasted_iota(jnp.int32, (rows, BLOCK), 1)
    valid_prev = sk >= t + jnp.where(n > 0, 0, BLOCK)
    valid_next = sk <= t - jnp.where(n < nb - 1, 0, BLOCK)
    scale2 = HEAD_DIM ** -0.5 * LOG2E

    for hk in range(N_KV_HEADS):
        c = slice(hk * HEAD_DIM, (hk + 1) * HEAD_DIM)
        kwin = jnp.concatenate([kp_ref[0, :, c], kc_ref[0, :, c], kn_ref[0, :, c]], axis=0)
        vwin = jnp.concatenate([vp_ref[0, :, c], vc_ref[0, :, c], vn_ref[0, :, c]], axis=0).astype(BF16)
        kr = _rope(kwin, cc_k, ss_k).astype(BF16)
        qg = jnp.concatenate(
            [q_ref[0, :, (hk * Q_PER_KV + g) * HEAD_DIM:(hk * Q_PER_KV + g + 1) * HEAD_DIM] for g in range(Q_PER_KV)],
            axis=0,
        )
        qr = _rope(qg, cc_q, ss_q).astype(BF16)
        s = lax.dot_general(qr, kr, (((1,), (1,)), ((), ())), preferred_element_type=F32) * scale2
        s = jnp.concatenate([jnp.where(valid_prev, s[:, :BLOCK], NEG_BIG), s[:, BLOCK:2 * BLOCK],
                             jnp.where(valid_next, s[:, 2 * BLOCK:], NEG_BIG)], axis=1)
        ps, dens = [], []
        for g in range(Q_PER_KV):
            sg = s[g * BLOCK:(g + 1) * BLOCK]
            sink2 = sink_ref[hk * Q_PER_KV + g] * LOG2E
            m = jnp.maximum(jnp.max(sg, axis=-1, keepdims=True), sink2)
            p = jnp.exp2(sg - m)
            dens.append(jnp.sum(p, axis=-1, keepdims=True) + jnp.exp2(sink2 - m))
            ps.append(p.astype(BF16))
        pv = _dot(jnp.concatenate(ps, axis=0), vwin)
        for g in range(Q_PER_KV):
            h = hk * Q_PER_KV + g
            o = pv[g * BLOCK:(g + 1) * BLOCK] / dens[g]
            o_ref[0, :, h * HEAD_DIM:(h + 1) * HEAD_DIM] = o.astype(o_ref.dtype)


def windowed_attention(proj3, sink, cc, ss):
    nseq, l, _ = proj3.shape
    nb = l // BLOCK
    kcol, vcol = OFF_K // KV_W, OFF_V // KV_W
    prev = lambda n: jnp.maximum(n - 1, 0)
    nxt = lambda n: jnp.minimum(n + 1, nb - 1)
    kv = lambda rowf, col: pl.BlockSpec((1, BLOCK, KV_W), lambda b, n: (b, rowf(n), col))
    ident = lambda n: n
    return pl.pallas_call(
        _attn_kernel,
        grid=(nseq, nb),
        in_specs=[
            pl.BlockSpec(memory_space=pltpu.SMEM),
            pl.BlockSpec((1, BLOCK, ATTN_W), lambda b, n: (b, n, OFF_Q // ATTN_W)),
            kv(prev, kcol), kv(ident, kcol), kv(nxt, kcol),
            kv(prev, vcol), kv(ident, vcol), kv(nxt, vcol),
            pl.BlockSpec((l, HEAD_DIM), lambda b, n: (0, 0)),
            pl.BlockSpec((l, HEAD_DIM), lambda b, n: (0, 0)),
        ],
        out_specs=pl.BlockSpec((1, BLOCK, ATTN_W), lambda b, n: (b, n, 0)),
        out_shape=jax.ShapeDtypeStruct((nseq, l, ATTN_W), BF16),
        compiler_params=_params("parallel", "arbitrary"),
        name="windowed_attention",
    )(sink, proj3, proj3, proj3, proj3, proj3, proj3, proj3, cc, ss)


HALO = SUBLANES
PAD = D_CONV // 2
CONV_PHASES = 4
CONV_SUB = 64
CONV_BLK = CONV_PHASES * CONV_SUB


def _conv_kernel(x_ref, w_ref, b_ref, o_ref, edge_ref, *res_ref):
    rows = x_ref.shape[1]
    nblk = rows // CONV_BLK
    w = [w_ref[j:j + 1, :] for j in range(D_CONV)]
    bias = b_ref[...]

    def emit(tap, out_base):
        for r in range(CONV_PHASES):
            acc = bias
            for j in range(D_CONV):
                acc = acc + tap(r + j - PAD) * w[j]
            y = _silu(acc)
            if res_ref:
                res_ref[0][pl.ds(out_base + r, CONV_SUB, stride=CONV_PHASES), :] = y
            else:
                o_ref[0, pl.ds(out_base + r, CONV_SUB, stride=CONV_PHASES), :] = y

    def body(i, c):
        base = i * CONV_BLK
        emit(lambda off: x_ref[0, pl.ds(base + off, CONV_SUB, stride=CONV_PHASES), :], base)
        return c

    lax.fori_loop(1, nblk - 1, body, 0)

    zeros = jnp.zeros((HALO, LANES), F32)
    edge_tap = lambda slot: (lambda off: edge_ref[slot, pl.ds(HALO + off, CONV_SUB, stride=CONV_PHASES), :])
    edge_ref[0, 0:HALO, :] = zeros
    if nblk == 1:
        edge_ref[0, HALO:HALO + rows, :] = x_ref[0]
        edge_ref[0, HALO + rows:, :] = zeros
        emit(edge_tap(0), 0)
    else:
        edge_ref[0, HALO:, :] = x_ref[0, 0:CONV_BLK + HALO, :]
        edge_ref[1, 0:HALO + CONV_BLK, :] = x_ref[0, rows - CONV_BLK - HALO:rows, :]
        edge_ref[1, HALO + CONV_BLK:, :] = zeros
        emit(edge_tap(0), 0)
        emit(edge_tap(1), rows - CONV_BLK)
    if res_ref:
        o_ref[0] = res_ref[0][...].astype(o_ref.dtype)


def conv_silu(proj3, conv_w, conv_b, *, ch_off, n_ch, out_dtype):
    nseq, l, _ = proj3.shape
    col0 = (OFF_XBC + ch_off) // LANES
    wcol0 = ch_off // LANES
    scratch = [pltpu.VMEM((2, CONV_BLK + 2 * HALO, LANES), F32)]
    if out_dtype != F32:
        scratch.append(pltpu.VMEM((l, LANES), F32))
    return pl.pallas_call(
        _conv_kernel,
        grid=(nseq, n_ch // LANES),
        in_specs=[
            pl.BlockSpec((1, l, LANES), lambda b, c: (b, 0, col0 + c)),
            pl.BlockSpec((D_CONV, LANES), lambda b, c: (0, wcol0 + c)),
            pl.BlockSpec((1, LANES), lambda b, c: (0, wcol0 + c)),
        ],
        out_specs=pl.BlockSpec((1, l, LANES), lambda b, c: (b, 0, c)),
        out_shape=jax.ShapeDtypeStruct((nseq, l, n_ch), out_dtype),
        scratch_shapes=scratch,
        compiler_params=_params("parallel", "parallel"),
        name="conv_silu",
    )(proj3, conv_w, conv_b)


DT_CHUNKS = 8
PAIRS = HEADS_PER_GROUP // 2
DH = 2 * SSD_HEADS
COL_AF, COL_AB, COL_GF, COL_GB = (k * HEADS_PER_GROUP for k in range(4))
COL_USED = 4 * HEADS_PER_GROUP
GROUP_LANES = SSD_HEADS // 2


def _split3(a):
    hi = a.astype(BF16)
    r = a - hi.astype(F32)
    mid = r.astype(BF16)
    lo = (r - mid.astype(F32)).astype(BF16)
    return hi, mid, lo


def _tri_sum(tri, a):
    hi, mid, lo = _split3(a)
    return _dot(tri, hi) + _dot(tri, mid) + _dot(tri, lo)


def _pair_rows(v, row_lo, lane_even):
    swap = lambda u: jnp.concatenate([u[CHUNK // 2:], u[:CHUNK // 2]], axis=0)
    from_prev_lane = swap(pltpu.roll(v, 1, 1))
    from_next_lane = swap(pltpu.roll(v, DH - 1, 1))
    w = jnp.where(row_lo, jnp.where(lane_even, v, from_prev_lane), jnp.where(lane_even, from_next_lane, v))
    return w.T


def _dt_kernel(raw_ref, bias_ref, alog_ref, col_ref, rowg_ref, rowdt_ref, tot_ref):
    neg_a2 = -jnp.exp(alog_ref[...]) * LOG2E
    bias = bias_ref[...]
    r = lax.broadcasted_iota(jnp.int32, (CHUNK, CHUNK), 0)
    c = lax.broadcasted_iota(jnp.int32, (CHUNK, CHUNK), 1)
    lower = (r >= c).astype(BF16)
    upper = (r <= c).astype(BF16)
    fwd_lane = c < SSD_HEADS
    row_lo = r < CHUNK // 2
    lane_even = (c & 1) == 0
    src0 = ((c >> 3) & 1) * SSD_HEADS + (c & (HEADS_PER_GROUP - 1))
    low_groups = (c & GROUP_LANES) == 0
    lane8 = lax.broadcasted_iota(jnp.int32, (DT_CHUNKS, DH), 1)
    totals = []
    for ci in range(DT_CHUNKS):
        rows = slice(ci * CHUNK, (ci + 1) * CHUNK)
        raw = raw_ref[rows, :] + bias
        dt = jnp.maximum(raw, 0.0) + jnp.log1p(jnp.exp(-jnp.abs(raw)))
        a2 = dt * neg_a2
        pre = _tri_sum(lower, a2)
        suf = _tri_sum(upper, a2)
        acum2 = jnp.where(fwd_lane, pre, suf)
        g = jnp.log2(dt) - acum2
        totals.append(jnp.where(fwd_lane[0:1], pre[CHUNK - 1:CHUNK], suf[0:1]))
        rowg_ref[ci] = _pair_rows(g, row_lo, lane_even)
        rowdt_ref[ci] = _pair_rows(dt, row_lo, lane_even)
        sources = (jnp.where(low_groups, acum2, pltpu.roll(g, GROUP_LANES, 1)),
                   jnp.where(low_groups, pltpu.roll(g, DH - GROUP_LANES, 1), acum2))
        for grp in range(SSD_GROUPS):
            hi = grp >= SSD_GROUPS // 2
            idx = src0 + grp * HEADS_PER_GROUP + jnp.where(c < COL_GF, 0, -GROUP_LANES if hi else GROUP_LANES)
            packed = jnp.take_along_axis(sources[hi], idx, axis=1)
            col_ref[grp, rows, :] = jnp.where(c < COL_USED, packed, 0.0)
    tot8 = jnp.concatenate(totals, axis=0)
    for d in range(2):
        for m in range(SSD_HEADS // 2):
            idx = jnp.where(lane8 < SSD_HEAD_DIM, d * SSD_HEADS + 2 * m, d * SSD_HEADS + 2 * m + 1)
            tot_ref[d, :, m * LANES:(m + 1) * LANES] = jnp.take_along_axis(tot8, idx, axis=1)


def dt_prepare(proj, dt_bias, a_log):
    t = proj.shape[0]
    rows = DT_CHUNKS * CHUNK
    assert t % rows == 0, "token count must be a multiple of DT_CHUNKS chunks"
    cps = DT_CHUNKS
    nchunks = t // CHUNK
    col = OFF_DT // DH
    return pl.pallas_call(
        _dt_kernel,
        grid=(t // rows,),
        in_specs=[
            pl.BlockSpec((rows, DH), lambda i: (i, col)),
            pl.BlockSpec((1, DH), lambda i: (0, 0)),
            pl.BlockSpec((1, DH), lambda i: (0, 0)),
        ],
        out_specs=[
            pl.BlockSpec((SSD_GROUPS, rows, LANES), lambda i: (0, i, 0)),
            pl.BlockSpec((cps, DH, CHUNK), lambda i: (i, 0, 0)),
            pl.BlockSpec((cps, DH, CHUNK), lambda i: (i, 0, 0)),
            pl.BlockSpec((2, cps, D_INNER), lambda i: (0, i, 0)),
        ],
        out_shape=[
            jax.ShapeDtypeStruct((SSD_GROUPS, t, LANES), F32),
            jax.ShapeDtypeStruct((nchunks, DH, CHUNK), F32),
            jax.ShapeDtypeStruct((nchunks, DH, CHUNK), F32),
            jax.ShapeDtypeStruct((2, nchunks, D_INNER), F32),
        ],
        compiler_params=_params("parallel"),
        name="dt_prepare",
    )(proj, dt_bias, a_log)


SSD_CPS = 16
HALF = CHUNK // 2


def _expand(colv, base, lane_lo):
    tiles = []
    for p in range(PAIRS):
        idx = jnp.where(lane_lo, base + 2 * p, base + 2 * p + 1)
        tiles.append(jnp.take_along_axis(colv, idx, axis=1))
    return jnp.concatenate(tiles, axis=1)


def _ssd_kernel(x_ref, b_ref, c_ref, z_ref, col_ref, rowgf_ref, rowgb_ref, rowdtf_ref,
                tot_ref, skip_ref, ng_ref, o_ref, sf_ref, sb_ref, sb_all_ref):
    ph = pl.program_id(2)
    cs = pl.program_id(3)
    ncs = pl.num_programs(3)
    cps = x_ref.shape[1] // CHUNK
    lane = lax.broadcasted_iota(jnp.int32, (CHUNK, LANES), 1)
    lane_lo = lane < SSD_HEAD_DIM

    def state_update(s_ref, xs, bc, g_x, tot_row):
        xw = (xs * jnp.exp2(tot_row + g_x)).astype(BF16)
        st = lax.dot_general(bc, xw, (((0,), (0,)), ((), ())), preferred_element_type=F32)
        s_ref[...] = s_ref[...] * jnp.exp2(tot_row) + st

    @pl.when(ph == 0)
    def _backward_states():
        @pl.when(cs == 0)
        def _():
            sb_ref[...] = jnp.zeros_like(sb_ref)

        for k in range(cps):
            ci = cps - 1 - k
            rows = slice(ci * CHUNK, (ci + 1) * CHUNK)
            chunk = (ncs - 1 - cs) * cps + ci
            sb_all_ref[chunk] = sb_ref[...].astype(BF16)
            gb = _expand(col_ref[0, 0, rows, :], COL_GB, lane_lo)
            state_update(sb_ref, x_ref[0, rows, :], b_ref[0, rows, :], gb, tot_ref[1, ci:ci + 1, :])

    @pl.when(ph == 1)
    def _forward_and_output():
        @pl.when(cs == 0)
        def _():
            sf_ref[...] = jnp.zeros_like(sf_ref)

        li = lax.broadcasted_iota(jnp.int32, (HALF, LANES), 0)
        s_in_half = lax.broadcasted_iota(jnp.int32, (HALF, LANES), 1) & (HALF - 1)
        fwd = li > s_in_half
        diag = li == s_in_half
        for ci in range(cps):
            rows = slice(ci * CHUNK, (ci + 1) * CHUNK)
            chunk = cs * cps + ci
            xs = x_ref[0, rows, :]
            xs16 = xs.astype(BF16)
            bc = b_ref[0, rows, :]
            cc = c_ref[0, rows, :]
            cb = lax.dot_general(cc, bc, (((1,), (1,)), ((), ())), preferred_element_type=F32)
            cb_rot = pltpu.roll(cb, HALF, 1)
            cbs = [jnp.where(lane_lo, cb, cb_rot), jnp.where(lane_lo, cb_rot, cb)]
            colv = col_ref[0, 0, rows, :]
            af = _expand(colv, COL_AF, lane_lo)
            ab = _expand(colv, COL_AB, lane_lo)
            gf = _expand(colv, COL_GF, lane_lo)

            ydiag = []
            for p in range(PAIRS):
                afp = af[:, p * LANES:(p + 1) * LANES]
                abp = ab[:, p * LANES:(p + 1) * LANES]
                tiles = []
                for half in (0, 1):
                    rr = slice(2 * p + half, 2 * p + half + 1)
                    rgf, rgb = rowgf_ref[0, ci, rr, :], rowgb_ref[0, ci, rr, :]
                    same = slice(half * HALF, (half + 1) * HALF)
                    e_same = jnp.exp2(jnp.where(fwd, afp[same] + rgf, abp[same] + rgb))
                    e_same = e_same + jnp.where(diag, rowdtf_ref[0, ci, rr, :], 0.0)
                    if half == 0:
                        e = jnp.concatenate([e_same, jnp.exp2(afp[HALF:] + rgf)], axis=0)
                    else:
                        e = jnp.concatenate([jnp.exp2(abp[:HALF] + rgb), e_same], axis=0)
                    tiles.append((cbs[half] * e).astype(BF16))
                xp = xs16[:, p * LANES:(p + 1) * LANES]
                zero = jnp.zeros_like(xp)
                x_lo = jnp.where(lane_lo, xp, zero)
                x_hi = jnp.where(lane_lo, zero, xp)
                rhs = jnp.concatenate([x_lo[:HALF], x_hi[:HALF], x_lo[HALF:], x_hi[HALF:]], axis=0)
                ydiag.append(_dot(jnp.concatenate(tiles, axis=1), rhs))
            y = jnp.concatenate(ydiag, axis=1)

            y = y + _dot(cc, sf_ref[...].astype(BF16)) * jnp.exp2(af)
            y = y + _dot(cc, sb_all_ref[chunk]) * jnp.exp2(ab)
            y = y + xs * skip_ref[...]

            state_update(sf_ref, xs, bc, gf, tot_ref[0, ci:ci + 1, :])

            z = z_ref[0, rows, :]
            y = y * _silu(z)
            ms = jnp.mean(y * y, axis=-1, keepdims=True)
            o_ref[0, rows, :] = ((y * lax.rsqrt(ms + EPS)) * ng_ref[...]).astype(o_ref.dtype)


def ssd_scan(xs, bc, proj3, colpack, rowg, rowdt, tot_x, skip_x, norm_g):
    nseq, l, _ = xs.shape
    nchunks = l // CHUNK
    cps = SSD_CPS if nchunks % SSD_CPS == 0 else SUBLANES
    assert nchunks % cps == 0, "sequence length must be a multiple of 8 chunks"
    ncs = nchunks // cps
    rows = cps * CHUNK
    blk = lambda ph, cs: jnp.where(ph == 0, ncs - 1 - cs, cs)
    fwd_only = lambda ph, cs: jnp.where(ph == 0, 0, cs)
    zcol = OFF_Z // GROUP_W
    row_spec = lambda d: pl.BlockSpec(
        (1, cps, HEADS_PER_GROUP, CHUNK), lambda b, g, ph, cs: (b, blk(ph, cs), d * SSD_GROUPS + g, 0))
    return pl.pallas_call(
        _ssd_kernel,
        grid=(nseq, SSD_GROUPS, 2, ncs),
        in_specs=[
            pl.BlockSpec((1, rows, GROUP_W), lambda b, g, ph, cs: (b, blk(ph, cs), g)),
            pl.BlockSpec((1, rows, D_STATE), lambda b, g, ph, cs: (b, blk(ph, cs), g)),
            pl.BlockSpec((1, rows, D_STATE), lambda b, g, ph, cs: (b, fwd_only(ph, cs), SSD_GROUPS + g)),
            pl.BlockSpec((1, rows, GROUP_W), lambda b, g, ph, cs: (b, fwd_only(ph, cs), zcol + g)),
            pl.BlockSpec((1, 1, rows, LANES), lambda b, g, ph, cs: (g, b, blk(ph, cs), 0)),
            row_spec(0), row_spec(1), row_spec(0),
            pl.BlockSpec((2, cps, GROUP_W), lambda b, g, ph, cs: (0, b * ncs + blk(ph, cs), g)),
            pl.BlockSpec((1, GROUP_W), lambda b, g, ph, cs: (0, g)),
            pl.BlockSpec((1, GROUP_W), lambda b, g, ph, cs: (0, g)),
        ],
        out_specs=pl.BlockSpec((1, rows, GROUP_W), lambda b, g, ph, cs: (b, fwd_only(ph, cs), g)),
        out_shape=jax.ShapeDtypeStruct((nseq, l, D_INNER), BF16),
        scratch_shapes=[
            pltpu.VMEM((D_STATE, GROUP_W), F32),
            pltpu.VMEM((D_STATE, GROUP_W), F32),
            pltpu.VMEM((nchunks, D_STATE, GROUP_W), BF16),
        ],
        compiler_params=_params("parallel", "parallel", "arbitrary", "arbitrary"),
        name="ssd_scan",
    )(xs, bc, bc, proj3, colpack, rowg, rowg, rowdt, tot_x, skip_x, norm_g)


def _rope_tables(l):
    half = HEAD_DIM // 2
    inv_freq = ROPE_THETA ** (-jnp.arange(half, dtype=F32) * 2.0 / HEAD_DIM)
    ang = jnp.arange(l, dtype=F32)[:, None] * inv_freq[None, :]
    cos, sin = jnp.cos(ang), jnp.sin(ang)
    return jnp.concatenate([cos, cos], axis=-1), jnp.concatenate([-sin, sin], axis=-1)


def _row_tile(t, want):
    while t % want:
        want //= 2
    return want


def _layer(layer, x_parts, nseq, l, cc, ss, w_in, conv_w, conv_b, a_log, dt_bias, skip_x, ssd_norm, attn_sink,
           w_out_attn, w_out_ssd, w_out, norm_mix, norm_ffn, w_gate_up, w_down):
    t = nseq * l
    nchunks = l // CHUNK
    tm = _row_tile(math.gcd(*[p.shape[0] for p in x_parts]), 1024)
    proj = norm_matmul(x_parts, norm_mix, w_in, layer, tm=tm, tn=IN_TILE_WIDE, out_dtype=F32)
    proj3 = proj.reshape(nseq, l, N_IN)

    attn = windowed_attention(proj3, attn_sink, cc, ss).reshape(t, ATTN_W)

    xs = conv_silu(proj3, conv_w, conv_b, ch_off=0, n_ch=D_INNER, out_dtype=F32)
    bcm = conv_silu(proj3, conv_w, conv_b, ch_off=D_INNER, n_ch=CONV_CH - D_INNER, out_dtype=BF16)

    colpack, rowg, rowdt, tot = dt_prepare(proj, dt_bias, a_log)
    colpack = colpack.reshape(SSD_GROUPS, nseq, l, LANES)
    rowg = rowg.reshape(nseq, nchunks, DH, CHUNK)
    rowdt = rowdt.reshape(nseq, nchunks, DH, CHUNK)
    ssd = ssd_scan(xs, bcm, proj3, colpack, rowg, rowdt, tot, skip_x, ssd_norm).reshape(t, D_INNER)

    mix = gated_mix(attn, ssd, w_out_attn, w_out_ssd, layer, proj, tm=tm, tn=512)
    x = resid_matmul(mix, w_out, layer, x_parts, tm=tm, tn=1024)
    act = norm_swiglu(x, norm_ffn, w_gate_up, layer, tm=tm, tn=512)
    x = resid_matmul(act, w_down, layer, [x], tm=tm, tn=512)
    return x


def _trunk(xs, w_in, conv_w, conv_b, a_log, dt_bias, d_skip, ssd_norm, attn_sink,
           w_out_attn, w_out_ssd, w_out, norm_mix, norm_ffn, w_gate_up, w_down):
    l = xs[0].shape[1]
    nseq = sum(x.shape[0] for x in xs)
    x_parts = [x.reshape(-1, D_MODEL) for x in xs]
    cc, ss = _rope_tables(l)
    w_in16 = w_in.astype(BF16)
    mats = [w.astype(BF16) for w in (w_out_attn, w_out_ssd, w_out)]
    ffn = [w.astype(BF16) for w in (w_gate_up, w_down)]
    for i in range(DEPTH):
        x_parts = [_layer(
            i, x_parts, nseq, l, cc, ss, w_in16, conv_w[i], conv_b[i][None, :],
            a_log[i].reshape(1, -1), dt_bias[i].reshape(1, -1), jnp.repeat(d_skip[i], SSD_HEAD_DIM)[None, :],
            ssd_norm[i][None, :], attn_sink[i], *mats, norm_mix[i][None, :], norm_ffn[i][None, :], *ffn)]
    return x_parts[0]


def kernel(x_prompt, x_sample, w_in, conv_w, conv_b, a_log, dt_bias, d_skip, ssd_norm, attn_sink,
           w_out_attn, w_out_ssd, w_out, norm_mix, norm_ffn, w_gate_up, w_down, final_norm):
    weights = (w_in, conv_w, conv_b, a_log, dt_bias, d_skip, ssd_norm, attn_sink,
               w_out_attn, w_out_ssd, w_out, norm_mix, norm_ffn, w_gate_up, w_down)
    g = final_norm[None, :]
    outs = []
    if x_prompt.shape[1] == x_sample.shape[1]:
        l = x_prompt.shape[1]
        x = _trunk([x_prompt, x_sample], *weights)
        row0 = 0
        for xin in (x_prompt, x_sample):
            n = xin.shape[0] * l
            outs.append(rmsnorm(x, g, tm=_row_tile(l, 512), row0=row0, nrows=n).reshape(xin.shape))
            row0 += n
    else:
        for xin in (x_prompt, x_sample):
            x = _trunk([xin], *weights)
            outs.append(rmsnorm(x, g, tm=_row_tile(x.shape[0], 512), row0=0, nrows=x.shape[0]).reshape(xin.shape))
    return tuple(outs)
```
